```python
import math
import jax, jax.numpy as jnp
from jax import lax
import numpy as np

D_MODEL = 1024
BATCH = 16
SEQ = 2048
DEPTH = 4

CHUNK = 64
Q_BLOCK = 128
SB_HEADS = 8
SB_HEAD_DIM = 64
SB_WIDTH = SB_HEADS * SB_HEAD_DIM
DA_HEADS = 4
DA_HEAD_DIM = 64
DA_V_DIM = 2 * DA_HEAD_DIM
DA_WIDTH = DA_HEADS * DA_V_DIM
MIX_WIDTH = SB_WIDTH + DA_WIDTH
IN_PROJ = 3 * SB_WIDTH + 2 * (DA_HEADS * 2 * DA_HEAD_DIM) + DA_WIDTH
D_FF = 2816
CONV_WIDTH = 3
N_MOD = 6
EPS = 1e-6
NEG_INF = -1e30

kernel_name = "chunk_causal_hybrid_sb_diff_trunk"


def rms_norm(x, g):
    xf = x.astype(jnp.float32)
    y = xf * lax.rsqrt(jnp.mean(xf * xf, axis=-1, keepdims=True) + EPS)
    return (y * g.astype(jnp.float32)).astype(x.dtype)


def alibi_slopes(n_heads):
    return jnp.array([2.0 ** (-8.0 * (h + 1) / n_heads) for h in range(n_heads)], dtype=jnp.float32)


def stick_breaking_attention(q, k, v):
    S, dh = q.shape[2], q.shape[3]
    scale = dh ** -0.5
    outs = []
    for i in range(S // Q_BLOCK):
        t0, end = i * Q_BLOCK, (i + 1) * Q_BLOCK
        qb, kb, vb = q[:, :, t0:end], k[:, :, :end], v[:, :, :end]
        z = jnp.einsum("bhqd,bhkd->bhqk", qb, kb).astype(jnp.float32) * scale
        t_pos = t0 + jnp.arange(Q_BLOCK)
        s_pos = jnp.arange(end)
        mask = s_pos[None, :] < t_pos[:, None]
        log_beta = jax.nn.log_sigmoid(z)
        log_rem = jnp.where(mask, jax.nn.log_sigmoid(-z), 0.0)
        suffix = lax.cumsum(log_rem, axis=3, reverse=True) - log_rem
        w = jnp.where(mask, jnp.exp(log_beta + suffix), 0.0)
        outs.append(jnp.einsum("bhqk,bhkd->bhqd", w.astype(vb.dtype), vb))
    return jnp.concatenate(outs, axis=2)


def differential_attention(q, k, v, lam, slopes):
    S, dh = q.shape[3], q.shape[4]
    scale = dh ** -0.5
    outs = []
    for i in range(S // Q_BLOCK):
        t0, end = i * Q_BLOCK, (i + 1) * Q_BLOCK
        qb, kb, vb = q[:, :, :, t0:end], k[:, :, :, :end], v[:, :, :end]
        t_pos = t0 + jnp.arange(Q_BLOCK)
        s_pos = jnp.arange(end)
        dist = jnp.abs(t_pos[:, None] - s_pos[None, :]).astype(jnp.float32)
        bias = -slopes[:, None, None] * dist
        mask = (s_pos[None, :] // CHUNK) <= (t_pos[:, None] // CHUNK)
        sc = jnp.einsum("bhmqd,bhmkd->bhmqk", qb, kb).astype(jnp.float32) * scale
        sc = jnp.where(mask, sc + bias[None, :, None], NEG_INF)
        p = jax.nn.softmax(sc, axis=-1)
        attn = p[:, :, 0] - lam * p[:, :, 1]
        outs.append(jnp.einsum("bhqk,bhkd->bhqd", attn.astype(vb.dtype), vb))
    return jnp.concatenate(outs, axis=2)


def causal_depthwise_conv(u, w, b):
    C = u.shape[-1]
    y = lax.conv_general_dilated(u, w[:, None, :], window_strides=(1,),
                                 padding=[(CONV_WIDTH - 1, 0)],
                                 dimension_numbers=("NWC", "WIO", "NWC"),
                                 feature_group_count=C)
    return y + b


def setup_inputs(seed: int = 0) -> dict:
    key = jax.random.key(seed)
    ks = jax.random.split(key, 20)
    f32 = jnp.float32
    nrm = lambda k, shape, s: jax.random.normal(k, shape, f32) * s
    return {
        "x": nrm(ks[0], (BATCH, SEQ, D_MODEL), 1.0),
        "c": nrm(ks[1], (BATCH, D_MODEL), 1.0),
        "ada_w": nrm(ks[2], (DEPTH, D_MODEL, N_MOD * D_MODEL), D_MODEL ** -0.5),
        "ada_b": nrm(ks[3], (DEPTH, N_MOD * D_MODEL), 0.01),
        "attn_pre_g": 1.0 + nrm(ks[4], (DEPTH, D_MODEL), 0.02),
        "attn_post_g": 1.0 + nrm(ks[5], (DEPTH, D_MODEL), 0.02),
        "w_in": nrm(ks[6], (DEPTH, D_MODEL, IN_PROJ), D_MODEL ** -0.5),
        "w_out": nrm(ks[7], (DEPTH, MIX_WIDTH, D_MODEL), MIX_WIDTH ** -0.5),
        "lambda_q1": nrm(ks[8], (DEPTH, DA_HEAD_DIM), 0.1),
        "lambda_k1": nrm(ks[9], (DEPTH, DA_HEAD_DIM), 0.1),
        "lambda_q2": nrm(ks[10], (DEPTH, DA_HEAD_DIM), 0.1),
        "lambda_k2": nrm(ks[11], (DEPTH, DA_HEAD_DIM), 0.1),
        "da_subln_g": 1.0 + nrm(ks[12], (DEPTH, DA_V_DIM), 0.02),
        "ffn_pre_g": 1.0 + nrm(ks[13], (DEPTH, D_MODEL), 0.02),
        "ffn_post_g": 1.0 + nrm(ks[14], (DEPTH, D_MODEL), 0.02),
        "w_up": nrm(ks[15], (DEPTH, D_MODEL, 2 * D_FF), D_MODEL ** -0.5),
        "conv_w": nrm(ks[16], (DEPTH, CONV_WIDTH, 2 * D_FF), CONV_WIDTH ** -0.5),
        "conv_b": nrm(ks[17], (DEPTH, 2 * D_FF), 0.01),
        "w_down": nrm(ks[18], (DEPTH, D_FF, D_MODEL), D_FF ** -0.5),
    }


def reference(x, c, ada_w, ada_b, attn_pre_g, attn_post_g, w_in, w_out,
              lambda_q1, lambda_k1, lambda_q2, lambda_k2, da_subln_g,
              ffn_pre_g, ffn_post_g, w_up, conv_w, conv_b, w_down):
    B, S, D = x.shape
    slopes = alibi_slopes(DA_HEADS)
    c_act = jax.nn.silu(c)
    n_daqk = DA_HEADS * 2 * DA_HEAD_DIM
    for l in range(DEPTH):
        mod = c_act @ ada_w[l] + ada_b[l]
        sh_a, sc_a, g_a, sh_f, sc_f, g_f = [m[:, None, :] for m in jnp.split(mod, N_MOD, axis=-1)]

        h = rms_norm(x, attn_pre_g[l]) * (1.0 + sc_a) + sh_a
        proj = h @ w_in[l]
        sb_q, sb_k, sb_v, da_q, da_k, da_v = jnp.split(
            proj, np.cumsum([SB_WIDTH, SB_WIDTH, SB_WIDTH, n_daqk, n_daqk]).tolist(), axis=-1)
        to_heads = lambda t, hd: t.reshape(B, S, -1, hd).transpose(0, 2, 1, 3)
        sb_out = stick_breaking_attention(to_heads(sb_q, SB_HEAD_DIM), to_heads(sb_k, SB_HEAD_DIM),
                                          to_heads(sb_v, SB_HEAD_DIM))
        sb_out = sb_out.transpose(0, 2, 1, 3).reshape(B, S, SB_WIDTH)

        lambda_init = 0.8 - 0.6 * math.exp(-0.3 * l)
        lam = (jnp.exp(jnp.sum(lambda_q1[l].astype(jnp.float32) * lambda_k1[l].astype(jnp.float32)))
               - jnp.exp(jnp.sum(lambda_q2[l].astype(jnp.float32) * lambda_k2[l].astype(jnp.float32)))
               + lambda_init)
        dq = da_q.reshape(B, S, DA_HEADS, 2, DA_HEAD_DIM).transpose(0, 2, 3, 1, 4)
        dk = da_k.reshape(B, S, DA_HEADS, 2, DA_HEAD_DIM).transpose(0, 2, 3, 1, 4)
        dv = to_heads(da_v, DA_V_DIM)
        da_out = differential_attention(dq, dk, dv, lam, slopes)
        da_out = rms_norm(da_out, da_subln_g[l]) * (1.0 - lambda_init)
        da_out = da_out.transpose(0, 2, 1, 3).reshape(B, S, DA_WIDTH)

        mixed = jnp.concatenate([sb_out, da_out], axis=-1) @ w_out[l]
        x = x + g_a * rms_norm(mixed, attn_post_g[l])

        h = rms_norm(x, ffn_pre_g[l]) * (1.0 + sc_f) + sh_f
        u = causal_depthwise_conv(h @ w_up[l], conv_w[l], conv_b[l])
        gate, val = jnp.split(u, 2, axis=-1)
        f = (jax.nn.silu(gate) * val) @ w_down[l]
        x = x + g_f * rms_norm(f, ffn_post_g[l])
    return x
```

```python
import functools
import math

import jax
import jax.numpy as jnp
from jax import lax
from jax.experimental import pallas as pl
from jax.experimental.pallas import tpu as pltpu

F32 = jnp.float32
BF16 = jnp.bfloat16

CHUNK = 64
SB_HEADS = 8
HEAD_DIM = 64
DA_HEADS = 4
N_MOD = 6
CONV_WIDTH = 3
EPS = 1e-6
NEG_INF = -1e30

LANES = 128
SB_PAIRS = SB_HEADS // 2
VMEM_LIMIT_BYTES = 56 * 1024 * 1024

SB_SKIP_THRESHOLD = 110.0


def _rms(x, g):
    return x * lax.rsqrt(jnp.mean(x * x, axis=-1, keepdims=True) + EPS) * g


def _ada_kernel(c_ref, w_ref, b_ref, o_ref):
    c = c_ref[...]
    c_act = (c * jax.nn.sigmoid(c)).astype(BF16)
    o_ref[0] = jnp.dot(c_act, w_ref[0].astype(BF16), preferred_element_type=F32) + b_ref[0]


def _ada_mod(c, ada_w, ada_b, *, tn=1536):
    depth, d, n = ada_w.shape
    b = c.shape[0]
    return pl.pallas_call(
        _ada_kernel,
        grid=(depth, n // tn),
        in_specs=[
            pl.BlockSpec((b, d), lambda l, j: (0, 0)),
            pl.BlockSpec((1, d, tn), lambda l, j: (l, 0, j)),
            pl.BlockSpec((1, 1, tn), lambda l, j: (l, 0, j)),
        ],
        out_specs=pl.BlockSpec((1, b, tn), lambda l, j: (l, 0, j)),
        out_shape=jax.ShapeDtypeStruct((depth, b, n), F32),
        compiler_params=pltpu.CompilerParams(
            dimension_semantics=("arbitrary", "arbitrary"), vmem_limit_bytes=VMEM_LIMIT_BYTES),
        name="ada_mod",
    )(c, ada_w, ada_b.reshape(depth, 1, n))


def _in_proj_kernel(x_ref, mod_ref, g_ref, w_ref, o_ref, *, n_chunk):
    m = mod_ref[0]
    h = (_rms(x_ref[0], g_ref[...]) * (1.0 + m[1:2]) + m[0:1]).astype(BF16)
    n = w_ref.shape[1]
    for n0 in range(0, n, n_chunk):
        o_ref[0, :, n0:n0 + n_chunk] = jnp.dot(
            h, w_ref[:, n0:n0 + n_chunk], preferred_element_type=F32).astype(BF16)


def _in_proj(x, mod_l, g, w, *, tm=512, n_chunk=512):
    b, s, d = x.shape
    n = w.shape[1]
    return pl.pallas_call(
        functools.partial(_in_proj_kernel, n_chunk=n_chunk),
        grid=(b, s // tm),
        in_specs=[
            pl.BlockSpec((1, tm, d), lambda bi, i: (bi, i, 0)),
            pl.BlockSpec((1, N_MOD, d), lambda bi, i: (bi, 0, 0)),
            pl.BlockSpec((1, d), lambda bi, i: (0, 0)),
            pl.BlockSpec((d, n), lambda bi, i: (0, 0)),
        ],
        out_specs=pl.BlockSpec((1, tm, n), lambda bi, i: (bi, i, 0)),
        out_shape=jax.ShapeDtypeStruct((b, s, n), BF16),
        compiler_params=pltpu.CompilerParams(
            dimension_semantics=("arbitrary", "arbitrary"), vmem_limit_bytes=VMEM_LIMIT_BYTES),
        name="in_proj",
    )(x, mod_l, g.reshape(1, d), w)


def _build_kt(k_ref, kt_ref, *, tk):
    nkb = kt_ref.shape[0]
    row = lax.broadcasted_iota(jnp.int32, (LANES, tk), 0)

    def body(j, carry):
        start = pl.multiple_of(j * tk, tk)
        kt = k_ref[0, pl.ds(start, tk), :].astype(F32).T
        lo = jnp.where(row < HEAD_DIM, kt, 0.0)
        hi = jnp.where(row >= HEAD_DIM, kt, 0.0)
        kt_ref[j] = jnp.concatenate([lo, hi], axis=1).astype(BF16)
        return carry

    lax.fori_loop(0, nkb, body, 0)


def _sb_kernel(q_ref, k_ref, v_ref, o_ref, kt_ref, vm_ref, *, t):
    qi = pl.program_id(2)
    nkb = kt_ref.shape[0]

    @pl.when(qi == 0)
    def _():
        _build_kt(k_ref, kt_ref, tk=t)
        lane = lax.broadcasted_iota(jnp.int32, (t, LANES), 1)

        def body(j, carry):
            start = pl.multiple_of(j * t, t)
            vj = v_ref[0, pl.ds(start, t), :]
            zero = jnp.zeros_like(vj)
            vm_ref[j] = jnp.concatenate(
                [jnp.where(lane < HEAD_DIM, vj, zero), jnp.where(lane >= HEAD_DIM, vj, zero)], axis=0)
            return carry

        lax.fori_loop(0, nkb, body, 0)

    q = q_ref[0] * jnp.asarray(HEAD_DIM ** -0.5, BF16)
    r = lax.broadcasted_iota(jnp.int32, (t, t), 0)
    c = lax.broadcasted_iota(jnp.int32, (t, t), 1)
    causal = c < r
    tri = (r >= c).astype(BF16)

    def tile(j, carries, acc, diag):
        zz = jnp.dot(q, kt_ref[j], preferred_element_type=F32)
        ws, new_carries = [], []
        for h in range(2):
            z = zz[:, h * t:(h + 1) * t]
            sp = jnp.maximum(z, 0.0) + jnp.log(1.0 + jnp.exp(-jnp.abs(z)))
            if diag:
                sp = jnp.where(causal, sp, 0.0)
            hi = sp.astype(BF16)
            lo = (sp - hi.astype(F32)).astype(BF16)
            cs = (jnp.dot(hi, tri, preferred_element_type=F32)
                  + jnp.dot(lo, tri, preferred_element_type=F32))
            w = jnp.exp(z - cs - carries[h])
            if diag:
                w = jnp.where(causal, w, 0.0)
            ws.append(w.astype(BF16))
            new_carries.append(carries[h] + cs[:, 0:1])
        acc = acc + jnp.dot(jnp.concatenate(ws, axis=1), vm_ref[j], preferred_element_type=F32)
        return tuple(new_carries), acc

    zero_col = jnp.zeros((t, 1), F32)
    carries, acc = tile(qi, (zero_col, zero_col), jnp.zeros((t, LANES), F32), True)

    def cond(state):
        j, c0, c1, _ = state
        return jnp.logical_and(j >= 0, jnp.minimum(jnp.min(c0), jnp.min(c1)) < SB_SKIP_THRESHOLD)

    def body(state):
        j, c0, c1, acc = state
        (c0, c1), acc = tile(j, (c0, c1), acc, False)
        return j - 1, c0, c1, acc

    _, _, _, acc = lax.while_loop(cond, body, (qi - 1, carries[0], carries[1], acc))
    o_ref[0] = acc.astype(BF16)


def _sb_attention(proj, *, t=128):
    b, s, _ = proj.shape
    nkb = s // t
    k_off = SB_PAIRS
    v_off = 2 * SB_PAIRS
    return pl.pallas_call(
        functools.partial(_sb_kernel, t=t),
        grid=(b, SB_PAIRS, s // t),
        in_specs=[
            pl.BlockSpec((1, t, LANES), lambda bi, hp, qi: (bi, qi, hp)),
            pl.BlockSpec((1, s, LANES), lambda bi, hp, qi: (bi, 0, k_off + hp)),
            pl.BlockSpec((1, s, LANES), lambda bi, hp, qi: (bi, 0, v_off + hp)),
        ],
        out_specs=pl.BlockSpec((1, t, LANES), lambda bi, hp, qi: (bi, qi, hp)),
        out_shape=jax.ShapeDtypeStruct((b, s, SB_PAIRS * LANES), BF16),
        scratch_shapes=[
            pltpu.VMEM((nkb, LANES, 2 * t), BF16),
            pltpu.VMEM((nkb, 2 * t, LANES), BF16),
        ],
        compiler_params=pltpu.CompilerParams(
            dimension_semantics=("arbitrary", "arbitrary", "arbitrary"), vmem_limit_bytes=VMEM_LIMIT_BYTES),
        name="sb_attention",
    )(proj, proj, proj)


def _da_kernel(scal_ref, lq1_ref, lk1_ref, lq2_ref, lk2_ref, g_ref, q_ref, k_ref, v_ref, o_ref, kt_ref, *, t):
    h = pl.program_id(1)
    qi = pl.program_id(2)

    @pl.when(qi == 0)
    def _():
        _build_kt(k_ref, kt_ref, tk=t)

    slope = jnp.float32(2.0 ** (-8.0 * DA_HEADS / DA_HEADS))
    for hh in range(DA_HEADS - 2, -1, -1):
        slope = jnp.where(h == hh, jnp.float32(2.0 ** (-8.0 * (hh + 1) / DA_HEADS)), slope)

    q = q_ref[0] * jnp.asarray(HEAD_DIM ** -0.5, BF16)
    r = lax.broadcasted_iota(jnp.int32, (t, t), 0)
    c = lax.broadcasted_iota(jnp.int32, (t, t), 1)
    rc = (r - c).astype(F32)
    chunk_mask = (c // CHUNK) <= (r // CHUNK)

    def tile(j, state, diag):
        ss = jnp.dot(q, kt_ref[j], preferred_element_type=F32)
        offset = ((qi - j) * t).astype(F32)
        bias = -slope * jnp.abs(rc + offset)
        start = pl.multiple_of(j * t, t)
        vj = v_ref[0, pl.ds(start, t), :]
        new_state = []
        for m in range(2):
            m_old, l_old, a_old = state[m]
            sc = ss[:, m * t:(m + 1) * t] + bias
            if diag:
                sc = jnp.where(chunk_mask, sc, NEG_INF)
            m_new = jnp.maximum(m_old, jnp.max(sc, axis=-1, keepdims=True))
            alpha = jnp.exp(m_old - m_new)
            p = jnp.exp(sc - m_new)
            l_new = alpha * l_old + jnp.sum(p, axis=-1, keepdims=True)
            a_new = alpha * a_old + jnp.dot(p.astype(BF16), vj, preferred_element_type=F32)
            new_state.append((m_new, l_new, a_new))
        return tuple(new_state)

    init_map = (jnp.full((t, 1), NEG_INF, F32), jnp.zeros((t, 1), F32), jnp.zeros((t, LANES), F32))
    state = lax.fori_loop(0, qi, lambda j, st: tile(j, st, False), (init_map, init_map))
    state = tile(qi, state, True)

    lambda_init = scal_ref[0]
    lam = (jnp.exp(jnp.sum(lq1_ref[...] * lk1_ref[...], axis=-1, keepdims=True))
           - jnp.exp(jnp.sum(lq2_ref[...] * lk2_ref[...], axis=-1, keepdims=True)) + lambda_init)
    (_, l0, a0), (_, l1, a1) = state
    o = a0 / l0 - lam * (a1 / l1)
    o_ref[0] = (_rms(o, g_ref[...]) * scal_ref[1]).astype(BF16)


def _da_attention(proj, scal, lq1, lk1, lq2, lk2, g, *, t=128):
    b, s, _ = proj.shape
    nkb = s // t
    q_off = 3 * SB_PAIRS
    k_off = q_off + DA_HEADS
    v_off = k_off + DA_HEADS
    vec = lambda n: pl.BlockSpec((1, n), lambda bi, h, qi: (0, 0))
    return pl.pallas_call(
        functools.partial(_da_kernel, t=t),
        grid=(b, DA_HEADS, s // t),
        in_specs=[
            pl.BlockSpec(memory_space=pltpu.SMEM),
            vec(HEAD_DIM), vec(HEAD_DIM), vec(HEAD_DIM), vec(HEAD_DIM), vec(LANES),
            pl.BlockSpec((1, t, LANES), lambda bi, h, qi: (bi, qi, q_off + h)),
            pl.BlockSpec((1, s, LANES), lambda bi, h, qi: (bi, 0, k_off + h)),
            pl.BlockSpec((1, s, LANES), lambda bi, h, qi: (bi, 0, v_off + h)),
        ],
        out_specs=pl.BlockSpec((1, t, LANES), lambda bi, h, qi: (bi, qi, h)),
        out_shape=jax.ShapeDtypeStruct((b, s, DA_HEADS * LANES), BF16),
        scratch_shapes=[pltpu.VMEM((nkb, LANES, 2 * t), BF16)],
        compiler_params=pltpu.CompilerParams(
            dimension_semantics=("arbitrary", "arbitrary", "arbitrary"), vmem_limit_bytes=VMEM_LIMIT_BYTES),
        name="da_attention",
    )(scal, lq1.reshape(1, -1), lk1.reshape(1, -1), lq2.reshape(1, -1), lk2.reshape(1, -1),
      g.reshape(1, -1), proj, proj, proj)


def _out_ffn_kernel(x_ref, sb_ref, da_ref, mod_ref, wout_ref, apost_ref, fpre_ref, wup_ref, cw_ref, cb_ref,
                    wdown_ref, fpost_ref, o_ref, carry_ref, ubuf_ref, *, tm, fc, halo):
    i = pl.program_id(1)
    d_ff = wdown_ref.shape[0]

    @pl.when(i == 0)
    def _():
        carry_ref[...] = jnp.zeros_like(carry_ref)

    m = mod_ref[0]
    n_sb = sb_ref.shape[2]
    mixed = (jnp.dot(sb_ref[0], wout_ref[0:n_sb, :], preferred_element_type=F32)
             + jnp.dot(da_ref[0], wout_ref[n_sb:, :], preferred_element_type=F32))
    x1 = x_ref[0] + m[2:3] * _rms(mixed, apost_ref[...])
    h2 = (_rms(x1, fpre_ref[...]) * (1.0 + m[4:5]) + m[3:4]).astype(BF16)

    f = jnp.zeros_like(x1)
    for ci in range(d_ff // fc):
        halves = []
        for part in range(2):
            col = part * d_ff + ci * fc
            u = jnp.dot(h2, wup_ref[:, col:col + fc], preferred_element_type=F32)
            ubuf_ref[0:halo, :] = carry_ref[:, col:col + fc]
            ubuf_ref[halo:halo + tm, :] = u
            carry_ref[:, col:col + fc] = u[tm - halo:tm]
            cw = cw_ref[:, col:col + fc]
            y = cb_ref[:, col:col + fc] + cw[CONV_WIDTH - 1:CONV_WIDTH] * u
            for tap in range(CONV_WIDTH - 1):
                back = CONV_WIDTH - 1 - tap
                y = y + cw[tap:tap + 1] * ubuf_ref[pl.ds(halo - back, tm), :]
            halves.append(y)
        gate, val = halves
        act = (gate * jax.nn.sigmoid(gate) * val).astype(BF16)
        f = f + jnp.dot(act, wdown_ref[ci * fc:(ci + 1) * fc, :], preferred_element_type=F32)
    o_ref[0] = x1 + m[5:6] * _rms(f, fpost_ref[...])


def _out_ffn(x, sb, da, mod_l, w_out, apost, fpre, w_up, conv_w, conv_b, w_down, fpost, *, tm=256, fc=256):
    b, s, d = x.shape
    d_ff = w_down.shape[0]
    halo = 8
    const = lambda shape: pl.BlockSpec(shape, lambda bi, i: (0,) * len(shape), pipeline_mode=pl.Buffered(1))
    row = lambda n: pl.BlockSpec((1, tm, n), lambda bi, i: (bi, i, 0))
    return pl.pallas_call(
        functools.partial(_out_ffn_kernel, tm=tm, fc=fc, halo=halo),
        grid=(b, s // tm),
        in_specs=[
            row(d), row(sb.shape[2]), row(da.shape[2]),
            pl.BlockSpec((1, N_MOD, d), lambda bi, i: (bi, 0, 0)),
            const(w_out.shape), const((1, d)), const((1, d)),
            const(w_up.shape), const(conv_w.shape), const((1, 2 * d_ff)),
            const(w_down.shape), const((1, d)),
        ],
        out_specs=row(d),
        out_shape=jax.ShapeDtypeStruct((b, s, d), F32),
        scratch_shapes=[
            pltpu.VMEM((halo, 2 * d_ff), F32),
            pltpu.VMEM((halo + tm, fc), F32),
        ],
        compiler_params=pltpu.CompilerParams(
            dimension_semantics=("arbitrary", "arbitrary"), vmem_limit_bytes=VMEM_LIMIT_BYTES),
        name="out_ffn",
    )(x, sb, da, mod_l, w_out, apost.reshape(1, d), fpre.reshape(1, d), w_up, conv_w,
      conv_b.reshape(1, 2 * d_ff), w_down, fpost.reshape(1, d))


def kernel(x, c, ada_w, ada_b, attn_pre_g, attn_post_g, w_in, w_out, lambda_q1, lambda_k1, lambda_q2, lambda_k2,
           da_subln_g, ffn_pre_g, ffn_post_g, w_up, conv_w, conv_b, w_down):
    depth = ada_w.shape[0]
    b, s, d = x.shape
    mod = _ada_mod(c, ada_w, ada_b).reshape(depth, b, N_MOD, d)
    w_in, w_out, w_up, w_down = (w.astype(BF16) for w in (w_in, w_out, w_up, w_down))
    for l in range(depth):
        proj = _in_proj(x, mod[l], attn_pre_g[l], w_in[l])
        sb = _sb_attention(proj)
        lambda_init = 0.8 - 0.6 * math.exp(-0.3 * l)
        scal = jnp.array([lambda_init, 1.0 - lambda_init], F32)
        da = _da_attention(proj, scal, lambda_q1[l], lambda_k1[l], lambda_q2[l], lambda_k2[l], da_subln_g[l])
        x = _out_ffn(x, sb, da, mod[l], w_out[l], attn_post_g[l], ffn_pre_g[l], w_up[l], conv_w[l], conv_b[l],
                     w_down[l], ffn_post_g[l])
    return x
```

```python
import functools
import math

import jax
import jax.numpy as jnp
from jax import lax
from jax.experimental import pallas as pl
from jax.experimental.pallas import tpu as pltpu

F32 = jnp.float32
BF16 = jnp.bfloat16

CHUNK = 64
SB_HEADS = 8
HEAD_DIM = 64
DA_HEADS = 4
N_MOD = 6
CONV_WIDTH = 3
EPS = 1e-6
NEG_INF = -1e30

LANES = 128
SB_PAIRS = SB_HEADS // 2
VMEM_LIMIT_BYTES = 56 * 1024 * 1024

SB_SKIP_THRESHOLD = 110.0


def _rms(x, g):
    return x * lax.rsqrt(jnp.mean(x * x, axis=-1, keepdims=True) + EPS) * g


def _ada_kernel(c_ref, w_ref, b_ref, o_ref):
    c = c_ref[...]
    c_act = (c * jax.nn.sigmoid(c)).astype(BF16)
    o_ref[0] = jnp.dot(c_act, w_ref[0].astype(BF16), preferred_element_type=F32) + b_ref[0]


def _ada_mod(c, ada_w, ada_b, *, tn=1536):
    depth, d, n = ada_w.shape
    b = c.shape[0]
    return pl.pallas_call(
        _ada_kernel,
        grid=(depth, n // tn),
        in_specs=[
            pl.BlockSpec((b, d), lambda l, j: (0, 0)),
            pl.BlockSpec((1, d, tn), lambda l, j: (l, 0, j)),
            pl.BlockSpec((1, 1, tn), lambda l, j: (l, 0, j)),
        ],
        out_specs=pl.BlockSpec((1, b, tn), lambda l, j: (l, 0, j)),
        out_shape=jax.ShapeDtypeStruct((depth, b, n), F32),
        compiler_params=pltpu.CompilerParams(
            dimension_semantics=("arbitrary", "arbitrary"), vmem_limit_bytes=VMEM_LIMIT_BYTES),
        name="ada_mod",
    )(c, ada_w, ada_b.reshape(depth, 1, n))


def _in_proj_kernel(x_ref, mod_ref, g_ref, w_ref, o_ref, *, n_chunk):
    m = mod_ref[0]
    h = (_rms(x_ref[0], g_ref[...]) * (1.0 + m[1:2]) + m[0:1]).astype(BF16)
    n = w_ref.shape[1]
    for n0 in range(0, n, n_chunk):
        o_ref[0, :, n0:n0 + n_chunk] = jnp.dot(
            h, w_ref[:, n0:n0 + n_chunk], preferred_element_type=F32).astype(BF16)


def _in_proj(x, mod_l, g, w, *, tm=512, n_chunk=512):
    b, s, d = x.shape
    n = w.shape[1]
    return pl.pallas_call(
        functools.partial(_in_proj_kernel, n_chunk=n_chunk),
        grid=(b, s // tm),
        in_specs=[
            pl.BlockSpec((1, tm, d), lambda bi, i: (bi, i, 0)),
            pl.BlockSpec((1, N_MOD, d), lambda bi, i: (bi, 0, 0)),
            pl.BlockSpec((1, d), lambda bi, i: (0, 0)),
            pl.BlockSpec((d, n), lambda bi, i: (0, 0)),
        ],
        out_specs=pl.BlockSpec((1, tm, n), lambda bi, i: (bi, i, 0)),
        out_shape=jax.ShapeDtypeStruct((b, s, n), BF16),
        compiler_params=pltpu.CompilerParams(
            dimension_semantics=("arbitrary", "arbitrary"), vmem_limit_bytes=VMEM_LIMIT_BYTES),
        name="in_proj",
    )(x, mod_l, g.reshape(1, d), w)


def _split_halves_t(blk):
    bt = blk.astype(F32).T
    row = lax.broadcasted_iota(jnp.int32, bt.shape, 0)
    return jnp.where(row < HEAD_DIM, bt, 0.0), jnp.where(row >= HEAD_DIM, bt, 0.0)


def _masked_qt(q_blk):
    lo, hi = _split_halves_t(q_blk * jnp.asarray(HEAD_DIM ** -0.5, BF16))
    return jnp.concatenate([lo, hi], axis=1).astype(BF16)


def _sb_kernel(q_ref, k_ref, v_ref, o_ref, vt_ref, qt_ref, acc_ref, c_ref, *, t, n_pairs):
    qi = pl.program_id(1)
    nkb = vt_ref.shape[1]

    @pl.when(qi == 0)
    def _():
        def body(j, carry):
            start = pl.multiple_of(j * t, t)
            for p in range(n_pairs):
                lo, hi = _split_halves_t(v_ref[0, pl.ds(start, t), p * LANES:(p + 1) * LANES])
                vt_ref[p, j] = jnp.concatenate([lo, hi], axis=1).astype(BF16)
            return carry

        lax.fori_loop(0, nkb, body, 0)

    for p in range(n_pairs):
        qt_ref[p] = _masked_qt(q_ref[0, :, p * LANES:(p + 1) * LANES])
    acc_ref[...] = jnp.zeros_like(acc_ref)
    c_ref[...] = jnp.zeros_like(c_ref)

    s_loc = lax.broadcasted_iota(jnp.int32, (t, t), 0)
    q_loc = lax.broadcasted_iota(jnp.int32, (t, t), 1)
    causal = s_loc < q_loc
    tri = (q_loc >= s_loc).astype(BF16)
    tri2 = jnp.concatenate([tri, tri], axis=1)

    def tile(j, diag):
        start = pl.multiple_of(j * t, t)
        cmin = None
        for p in range(n_pairs):
            kj = k_ref[0, pl.ds(start, t), p * LANES:(p + 1) * LANES]
            zz = jnp.dot(kj, qt_ref[p], preferred_element_type=F32)
            ws = []
            for h in range(2):
                z = zz[:, h * t:(h + 1) * t]
                sp = jnp.maximum(z, 0.0) + jnp.log(1.0 + jnp.exp(-jnp.abs(z)))
                if diag:
                    sp = jnp.where(causal, sp, 0.0)
                hi = sp.astype(BF16)
                lo = (sp - hi.astype(F32)).astype(BF16)
                cs = jnp.dot(tri2, jnp.concatenate([hi, lo], axis=0), preferred_element_type=F32) + c_ref[2 * p + h]
                w = jnp.exp(z - cs)
                if diag:
                    w = jnp.where(causal, w, 0.0)
                ws.append(w.astype(BF16))
                c_new = cs[0:1, :]
                c_ref[2 * p + h] = c_new
                cmin = c_new if cmin is None else jnp.minimum(cmin, c_new)
            acc_ref[p] += jnp.dot(vt_ref[p, j], jnp.concatenate(ws, axis=0), preferred_element_type=F32)
        return jnp.min(cmin)

    first = tile(qi, True)

    def cond(state):
        j, smallest = state
        return jnp.logical_and(j >= 0, smallest < SB_SKIP_THRESHOLD)

    lax.while_loop(cond, lambda state: (state[0] - 1, tile(state[0], False)), (qi - 1, first))

    for p in range(n_pairs):
        o_ref[0, :, p * LANES:(p + 1) * LANES] = acc_ref[p].T.astype(BF16)


def _sb_attention(proj, *, t=128, n_pairs=SB_PAIRS):
    b, s, _ = proj.shape
    nkb = s // t
    w = n_pairs * LANES
    groups = SB_PAIRS // n_pairs
    return pl.pallas_call(
        functools.partial(_sb_kernel, t=t, n_pairs=n_pairs),
        grid=(b * groups, s // t),
        in_specs=[
            pl.BlockSpec((1, t, w), lambda bg, qi: (bg // groups, qi, bg % groups)),
            pl.BlockSpec((1, s, w), lambda bg, qi: (bg // groups, 0, groups + bg % groups)),
            pl.BlockSpec((1, s, w), lambda bg, qi: (bg // groups, 0, 2 * groups + bg % groups)),
        ],
        out_specs=pl.BlockSpec((1, t, w), lambda bg, qi: (bg // groups, qi, bg % groups)),
        out_shape=jax.ShapeDtypeStruct((b, s, SB_PAIRS * LANES), BF16),
        scratch_shapes=[
            pltpu.VMEM((n_pairs, nkb, LANES, 2 * t), BF16),
            pltpu.VMEM((n_pairs, LANES, 2 * t), BF16),
            pltpu.VMEM((n_pairs, LANES, t), F32),
            pltpu.VMEM((2 * n_pairs, 1, t), F32),
        ],
        compiler_params=pltpu.CompilerParams(
            dimension_semantics=("arbitrary", "arbitrary"), vmem_limit_bytes=VMEM_LIMIT_BYTES),
        name="sb_attention",
    )(proj, proj, proj)


def _da_kernel(scal_ref, lq1_ref, lk1_ref, lq2_ref, lk2_ref, g_ref, q_ref, k_ref, v_ref, o_ref,
               vt_ref, qt_ref, acc_ref, m_ref, l_ref, *, t, n_heads, head0_of_group):
    qi = pl.program_id(1)
    nkb = vt_ref.shape[1]
    group = pl.program_id(0) % len(head0_of_group)

    @pl.when(qi == 0)
    def _():
        def body(j, carry):
            start = pl.multiple_of(j * t, t)
            for h in range(n_heads):
                vt_ref[h, j] = v_ref[0, pl.ds(start, t), h * LANES:(h + 1) * LANES].astype(F32).T.astype(BF16)
            return carry

        lax.fori_loop(0, nkb, body, 0)

    for h in range(n_heads):
        qt_ref[h] = _masked_qt(q_ref[0, :, h * LANES:(h + 1) * LANES])
    acc_ref[...] = jnp.zeros_like(acc_ref)
    l_ref[...] = jnp.zeros_like(l_ref)
    m_ref[...] = jnp.full_like(m_ref, NEG_INF)

    s_loc = lax.broadcasted_iota(jnp.int32, (t, t), 0)
    q_loc = lax.broadcasted_iota(jnp.int32, (t, t), 1)
    d_loc = (q_loc - s_loc).astype(F32)
    chunk_mask = (s_loc // CHUNK) <= (q_loc // CHUNK)

    def neg_slope(h):
        vals = [-(2.0 ** (-8.0 * (h0 + h + 1) / DA_HEADS)) for h0 in head0_of_group]
        out = jnp.float32(vals[-1])
        for gi in range(len(vals) - 2, -1, -1):
            out = jnp.where(group == gi, jnp.float32(vals[gi]), out)
        return out

    def tile(j, diag):
        start = pl.multiple_of(j * t, t)
        for h in range(n_heads):
            kj = k_ref[0, pl.ds(start, t), h * LANES:(h + 1) * LANES]
            ss = jnp.dot(kj, qt_ref[h], preferred_element_type=F32)
            if diag:
                bias = neg_slope(h) * jnp.abs(d_loc)
            else:
                bias = neg_slope(h) * (d_loc + ((qi - j) * t).astype(F32))
            for m in range(2):
                idx = 2 * h + m
                sc = ss[:, m * t:(m + 1) * t] + bias
                if diag:
                    sc = jnp.where(chunk_mask, sc, NEG_INF)
                m_old = m_ref[idx]
                m_new = jnp.maximum(m_old, jnp.max(sc, axis=0, keepdims=True))
                alpha = jnp.exp(m_old - m_new)
                p = jnp.exp(sc - m_new)
                l_ref[idx] = alpha * l_ref[idx] + jnp.sum(p, axis=0, keepdims=True)
                m_ref[idx] = m_new
                acc_ref[idx] = alpha * acc_ref[idx] + jnp.dot(
                    vt_ref[h, j], p.astype(BF16), preferred_element_type=F32)

    def body(j, carry):
        tile(j, False)
        return carry

    lax.fori_loop(0, qi, body, 0)
    tile(qi, True)

    lambda_init = scal_ref[0]
    lam = (jnp.exp(jnp.sum(lq1_ref[...] * lk1_ref[...], axis=-1, keepdims=True))
           - jnp.exp(jnp.sum(lq2_ref[...] * lk2_ref[...], axis=-1, keepdims=True)) + lambda_init)
    for h in range(n_heads):
        o_t = (acc_ref[2 * h] * (1.0 / l_ref[2 * h])
               - lam * (acc_ref[2 * h + 1] * (1.0 / l_ref[2 * h + 1])))
        o_ref[0, :, h * LANES:(h + 1) * LANES] = (_rms(o_t.T, g_ref[...]) * scal_ref[1]).astype(BF16)


def _da_attention(proj, scal, lq1, lk1, lq2, lk2, g, *, t=128, n_heads=DA_HEADS):
    b, s, _ = proj.shape
    nkb = s // t
    w = n_heads * LANES
    groups = DA_HEADS // n_heads
    q_off = (3 * SB_PAIRS * LANES) // w
    k_off = q_off + groups
    v_off = k_off + groups
    vec = lambda n: pl.BlockSpec((1, n), lambda bg, qi: (0, 0))
    return pl.pallas_call(
        functools.partial(_da_kernel, t=t, n_heads=n_heads,
                          head0_of_group=tuple(gi * n_heads for gi in range(groups))),
        grid=(b * groups, s // t),
        in_specs=[
            pl.BlockSpec(memory_space=pltpu.SMEM),
            vec(HEAD_DIM), vec(HEAD_DIM), vec(HEAD_DIM), vec(HEAD_DIM), vec(LANES),
            pl.BlockSpec((1, t, w), lambda bg, qi: (bg // groups, qi, q_off + bg % groups)),
            pl.BlockSpec((1, s, w), lambda bg, qi: (bg // groups, 0, k_off + bg % groups)),
            pl.BlockSpec((1, s, w), lambda bg, qi: (bg // groups, 0, v_off + bg % groups)),
        ],
        out_specs=pl.BlockSpec((1, t, w), lambda bg, qi: (bg // groups, qi, bg % groups)),
        out_shape=jax.ShapeDtypeStruct((b, s, DA_HEADS * LANES), BF16),
        scratch_shapes=[
            pltpu.VMEM((n_heads, nkb, LANES, t), BF16),
            pltpu.VMEM((n_heads, LANES, 2 * t), BF16),
            pltpu.VMEM((2 * n_heads, LANES, t), F32),
            pltpu.VMEM((2 * n_heads, 1, t), F32),
            pltpu.VMEM((2 * n_heads, 1, t), F32),
        ],
        compiler_params=pltpu.CompilerParams(
            dimension_semantics=("arbitrary", "arbitrary"), vmem_limit_bytes=VMEM_LIMIT_BYTES),
        name="da_attention",
    )(scal, lq1.reshape(1, -1), lk1.reshape(1, -1), lq2.reshape(1, -1), lk2.reshape(1, -1),
      g.reshape(1, -1), proj, proj, proj)


def _out_ffn_kernel(x_ref, sb_ref, da_ref, mod_ref, wout_ref, apost_ref, fpre_ref, wup_ref, cw_ref, cb_ref,
                    wdown_ref, fpost_ref, o_ref, carry_ref, ubuf_ref, *, tm, fc, halo):
    i = pl.program_id(1)
    d_ff = wdown_ref.shape[0]

    @pl.when(i == 0)
    def _():
        carry_ref[...] = jnp.zeros_like(carry_ref)

    m = mod_ref[0]
    n_sb = sb_ref.shape[2]
    mixed = (jnp.dot(sb_ref[0], wout_ref[0:n_sb, :], preferred_element_type=F32)
             + jnp.dot(da_ref[0], wout_ref[n_sb:, :], preferred_element_type=F32))
    x1 = x_ref[0] + m[2:3] * _rms(mixed, apost_ref[...])
    h2 = (_rms(x1, fpre_ref[...]) * (1.0 + m[4:5]) + m[3:4]).astype(BF16)

    f = jnp.zeros_like(x1)
    for ci in range(d_ff // fc):
        halves = []
        for part in range(2):
            col = part * d_ff + ci * fc
            u = jnp.dot(h2, wup_ref[:, col:col + fc], preferred_element_type=F32)
            ubuf_ref[0:halo, :] = carry_ref[:, col:col + fc]
            ubuf_ref[halo:halo + tm, :] = u
            carry_ref[:, col:col + fc] = u[tm - halo:tm]
            cw = cw_ref[:, col:col + fc]
            y = cb_ref[:, col:col + fc] + cw[CONV_WIDTH - 1:CONV_WIDTH] * u
            for tap in range(CONV_WIDTH - 1):
                back = CONV_WIDTH - 1 - tap
                y = y + cw[tap:tap + 1] * ubuf_ref[pl.ds(halo - back, tm), :]
            halves.append(y)
        gate, val = halves
        act = (gate * jax.nn.sigmoid(gate) * val).astype(BF16)
        f = f + jnp.dot(act, wdown_ref[ci * fc:(ci + 1) * fc, :], preferred_element_type=F32)
    o_ref[0] = x1 + m[5:6] * _rms(f, fpost_ref[...])


def _out_ffn(x, sb, da, mod_l, w_out, apost, fpre, w_up, conv_w, conv_b, w_down, fpost, *, tm=256, fc=256):
    b, s, d = x.shape
    d_ff = w_down.shape[0]
    halo = 8
    const = lambda shape: pl.BlockSpec(shape, lambda bi, i: (0,) * len(shape), pipeline_mode=pl.Buffered(1))
    row = lambda n: pl.BlockSpec((1, tm, n), lambda bi, i: (bi, i, 0))
    return pl.pallas_call(
        functools.partial(_out_ffn_kernel, tm=tm, fc=fc, halo=halo),
        grid=(b, s // tm),
        in_specs=[
            row(d), row(sb.shape[2]), row(da.shape[2]),
            pl.BlockSpec((1, N_MOD, d), lambda bi, i: (bi, 0, 0)),
            const(w_out.shape), const((1, d)), const((1, d)),
            const(w_up.shape), const(conv_w.shape), const((1, 2 * d_ff)),
            const(w_down.shape), const((1, d)),
        ],
        out_specs=row(d),
        out_shape=jax.ShapeDtypeStruct((b, s, d), F32),
        scratch_shapes=[
            pltpu.VMEM((halo, 2 * d_ff), F32),
            pltpu.VMEM((halo + tm, fc), F32),
        ],
        compiler_params=pltpu.CompilerParams(
            dimension_semantics=("arbitrary", "arbitrary"), vmem_limit_bytes=VMEM_LIMIT_BYTES),
        name="out_ffn",
    )(x, sb, da, mod_l, w_out, apost.reshape(1, d), fpre.reshape(1, d), w_up, conv_w,
      conv_b.reshape(1, 2 * d_ff), w_down, fpost.reshape(1, d))


def kernel(x, c, ada_w, ada_b, attn_pre_g, attn_post_g, w_in, w_out, lambda_q1, lambda_k1, lambda_q2, lambda_k2,
           da_subln_g, ffn_pre_g, ffn_post_g, w_up, conv_w, conv_b, w_down):
    depth = ada_w.shape[0]
    b, s, d = x.shape
    mod = _ada_mod(c, ada_w, ada_b).reshape(depth, b, N_MOD, d)
    w_in, w_out, w_up, w_down = (w.astype(BF16) for w in (w_in, w_out, w_up, w_down))
    for l in range(depth):
        proj = _in_proj(x, mod[l], attn_pre_g[l], w_in[l])
        sb = _sb_attention(proj)
        lambda_init = 0.8 - 0.6 * math.exp(-0.3 * l)
        scal = jnp.array([lambda_init, 1.0 - lambda_init], F32)
        da = _da_attention(proj, scal, lambda_q1[l], lambda_k1[l], lambda_q2[l], lambda_k2[l], da_subln_g[l])
        x = _out_ffn(x, sb, da, mod[l], w_out[l], attn_post_g[l], ffn_pre_g[l], w_up[l], conv_w[l], conv_b[l],
                     w_down[l], ffn_post_g[l])
    return x
```

```python
import functools
import math

import jax
import jax.numpy as jnp
from jax import lax
from jax.experimental import pallas as pl
from jax.experimental.pallas import tpu as pltpu

F32 = jnp.float32
BF16 = jnp.bfloat16

CHUNK = 64
SB_HEADS = 8
HEAD_DIM = 64
DA_HEADS = 4
N_MOD = 6
CONV_WIDTH = 3
EPS = 1e-6
NEG_INF = -1e30

LANES = 128
SB_PAIRS = SB_HEADS // 2
VMEM_LIMIT_BYTES = 56 * 1024 * 1024

SB_SKIP_THRESHOLD = 110.0
SB_SKEW = 2


def _rms(x, g):
    return x * lax.rsqrt(jnp.mean(x * x, axis=-1, keepdims=True) + EPS) * g


def _ada_kernel(c_ref, w_ref, b_ref, o_ref):
    c = c_ref[...]
    c_act = (c * jax.nn.sigmoid(c)).astype(BF16)
    o_ref[0] = jnp.dot(c_act, w_ref[0].astype(BF16), preferred_element_type=F32) + b_ref[0]


def _ada_mod(c, ada_w, ada_b, *, tn=1536):
    depth, d, n = ada_w.shape
    b = c.shape[0]
    return pl.pallas_call(
        _ada_kernel,
        grid=(depth, n // tn),
        in_specs=[
            pl.BlockSpec((b, d), lambda l, j: (0, 0)),
            pl.BlockSpec((1, d, tn), lambda l, j: (l, 0, j)),
            pl.BlockSpec((1, 1, tn), lambda l, j: (l, 0, j)),
        ],
        out_specs=pl.BlockSpec((1, b, tn), lambda l, j: (l, 0, j)),
        out_shape=jax.ShapeDtypeStruct((depth, b, n), F32),
        compiler_params=pltpu.CompilerParams(
            dimension_semantics=("arbitrary", "arbitrary"), vmem_limit_bytes=VMEM_LIMIT_BYTES),
        name="ada_mod",
    )(c, ada_w, ada_b.reshape(depth, 1, n))


def _in_proj_kernel(x_ref, mod_ref, g_ref, w_ref, o_ref, *, n_chunk):
    m = mod_ref[0]
    h = (_rms(x_ref[0], g_ref[...]) * (1.0 + m[1:2]) + m[0:1]).astype(BF16)
    n = w_ref.shape[1]
    for n0 in range(0, n, n_chunk):
        o_ref[0, :, n0:n0 + n_chunk] = jnp.dot(
            h, w_ref[:, n0:n0 + n_chunk], preferred_element_type=F32).astype(BF16)


def _in_proj(x, mod_l, g, w, *, tm=512, n_chunk=512):
    b, s, d = x.shape
    n = w.shape[1]
    return pl.pallas_call(
        functools.partial(_in_proj_kernel, n_chunk=n_chunk),
        grid=(b, s // tm),
        in_specs=[
            pl.BlockSpec((1, tm, d), lambda bi, i: (bi, i, 0)),
            pl.BlockSpec((1, N_MOD, d), lambda bi, i: (bi, 0, 0)),
            pl.BlockSpec((1, d), lambda bi, i: (0, 0)),
            pl.BlockSpec((d, n), lambda bi, i: (0, 0)),
        ],
        out_specs=pl.BlockSpec((1, tm, n), lambda bi, i: (bi, i, 0)),
        out_shape=jax.ShapeDtypeStruct((b, s, n), BF16),
        compiler_params=pltpu.CompilerParams(
            dimension_semantics=("arbitrary", "arbitrary"), vmem_limit_bytes=VMEM_LIMIT_BYTES),
        name="in_proj",
    )(x, mod_l, g.reshape(1, d), w)


def _split_halves_t(blk):
    bt = blk.astype(F32).T
    row = lax.broadcasted_iota(jnp.int32, bt.shape, 0)
    return jnp.where(row < HEAD_DIM, bt, 0.0), jnp.where(row >= HEAD_DIM, bt, 0.0)


def _masked_qt(q_blk):
    lo, hi = _split_halves_t(q_blk * jnp.asarray(HEAD_DIM ** -0.5, BF16))
    return jnp.concatenate([lo, hi], axis=1).astype(BF16)


def _sb_kernel(q_ref, k_ref, v_ref, o_ref, vt_ref, qt_ref, acc_ref, c_ref, *, t, n_pairs):
    qi = pl.program_id(1)
    nkb = vt_ref.shape[1]

    @pl.when(qi == 0)
    def _():
        def body(j, carry):
            start = pl.multiple_of(j * t, t)
            for p in range(n_pairs):
                lo, hi = _split_halves_t(v_ref[0, pl.ds(start, t), p * LANES:(p + 1) * LANES])
                vt_ref[p, j] = jnp.concatenate([lo, hi], axis=1).astype(BF16)
            return carry

        lax.fori_loop(0, nkb, body, 0)

    for p in range(n_pairs):
        qt_ref[p] = _masked_qt(q_ref[0, :, p * LANES:(p + 1) * LANES])
    acc_ref[...] = jnp.zeros_like(acc_ref)
    c_ref[...] = jnp.zeros_like(c_ref)

    s_loc = lax.broadcasted_iota(jnp.int32, (t, t), 0)
    q_loc = lax.broadcasted_iota(jnp.int32, (t, t), 1)
    causal = s_loc < q_loc
    tri = (q_loc >= s_loc).astype(BF16)

    def scores(j):
        start = pl.multiple_of(j * t, t)
        return tuple(jnp.dot(k_ref[0, pl.ds(start, t), p * LANES:(p + 1) * LANES], qt_ref[p],
                             preferred_element_type=F32) for p in range(n_pairs))

    def add_pv(j, ws):
        for p in range(n_pairs):
            acc_ref[p] += jnp.dot(vt_ref[p, j], ws[p], preferred_element_type=F32)

    def weights(zz, diag):
        n = 2 * n_pairs
        z_of = lambda i: zz[i // 2][:, (i % 2) * t:(i % 2 + 1) * t]
        suffix, ws, cmin = [None] * n, [None] * n, None
        for step in range(n + SB_SKEW):
            if step < n:
                z = z_of(step)
                sp = jnp.where(z > 80.0, z, jnp.log(1.0 + jnp.exp(z)))
                if diag:
                    sp = jnp.where(causal, sp, 0.0)
                suffix[step] = jnp.dot(tri, sp.astype(BF16), preferred_element_type=F32)
            i = step - SB_SKEW
            if i >= 0:
                cs = suffix[i] + c_ref[i]
                w = jnp.exp(z_of(i) - cs)
                if diag:
                    w = jnp.where(causal, w, 0.0)
                ws[i] = w.astype(BF16)
                c_new = cs[0:1, :]
                c_ref[i] = c_new
                cmin = c_new if cmin is None else jnp.minimum(cmin, c_new)
        pairs = tuple(jnp.concatenate([ws[2 * p], ws[2 * p + 1]], axis=0) for p in range(n_pairs))
        return pairs, jnp.min(cmin)

    zz_diag = scores(qi)
    zz_next = scores(jnp.maximum(qi - 1, 0))
    ws_diag, smallest = weights(zz_diag, True)

    def cond(state):
        j, smallest = state[0], state[1]
        return jnp.logical_and(j >= 0, smallest < SB_SKIP_THRESHOLD)

    def body(state):
        j, _, zz, ws_prev = state
        add_pv(j + 1, ws_prev)
        zz_next = scores(jnp.maximum(j - 1, 0))
        ws, smallest = weights(zz, False)
        return j - 1, smallest, zz_next, ws

    j_end, _, _, ws_last = lax.while_loop(cond, body, (qi - 1, smallest, zz_next, ws_diag))
    add_pv(j_end + 1, ws_last)

    for p in range(n_pairs):
        o_ref[0, :, p * LANES:(p + 1) * LANES] = acc_ref[p].T.astype(BF16)


def _sb_attention(proj, *, t=128, n_pairs=SB_PAIRS):
    b, s, _ = proj.shape
    nkb = s // t
    w = n_pairs * LANES
    groups = SB_PAIRS // n_pairs
    return pl.pallas_call(
        functools.partial(_sb_kernel, t=t, n_pairs=n_pairs),
        grid=(b * groups, s // t),
        in_specs=[
            pl.BlockSpec((1, t, w), lambda bg, qi: (bg // groups, qi, bg % groups)),
            pl.BlockSpec((1, s, w), lambda bg, qi: (bg // groups, 0, groups + bg % groups)),
            pl.BlockSpec((1, s, w), lambda bg, qi: (bg // groups, 0, 2 * groups + bg % groups)),
        ],
        out_specs=pl.BlockSpec((1, t, w), lambda bg, qi: (bg // groups, qi, bg % groups)),
        out_shape=jax.ShapeDtypeStruct((b, s, SB_PAIRS * LANES), BF16),
        scratch_shapes=[
            pltpu.VMEM((n_pairs, nkb, LANES, 2 * t), BF16),
            pltpu.VMEM((n_pairs, LANES, 2 * t), BF16),
            pltpu.VMEM((n_pairs, LANES, t), F32),
            pltpu.VMEM((2 * n_pairs, 1, t), F32),
        ],
        compiler_params=pltpu.CompilerParams(
            dimension_semantics=("arbitrary", "arbitrary"), vmem_limit_bytes=VMEM_LIMIT_BYTES),
        name="sb_attention",
    )(proj, proj, proj)


def _da_kernel(scal_ref, lq1_ref, lk1_ref, lq2_ref, lk2_ref, g_ref, q_ref, k_ref, v_ref, o_ref,
               vt_ref, qt_ref, acc_ref, m_ref, l_ref, b0_ref, *, t, n_heads, head0_of_group):
    qi = pl.program_id(1)
    nkb = vt_ref.shape[1]
    group = pl.program_id(0) % len(head0_of_group)

    @pl.when(qi == 0)
    def _():
        def body(j, carry):
            start = pl.multiple_of(j * t, t)
            for h in range(n_heads):
                vt_ref[h, j] = v_ref[0, pl.ds(start, t), h * LANES:(h + 1) * LANES].astype(F32).T.astype(BF16)
            return carry

        lax.fori_loop(0, nkb, body, 0)

    for h in range(n_heads):
        qt_ref[h] = _masked_qt(q_ref[0, :, h * LANES:(h + 1) * LANES])
    acc_ref[...] = jnp.zeros_like(acc_ref)
    l_ref[...] = jnp.zeros_like(l_ref)
    m_ref[...] = jnp.full_like(m_ref, NEG_INF)

    s_loc = lax.broadcasted_iota(jnp.int32, (t, t), 0)
    q_loc = lax.broadcasted_iota(jnp.int32, (t, t), 1)
    d_loc = (q_loc - s_loc).astype(F32)
    chunk_mask = (s_loc // CHUNK) <= (q_loc // CHUNK)

    def neg_slope(h):
        vals = [-(2.0 ** (-8.0 * (h0 + h + 1) / DA_HEADS)) for h0 in head0_of_group]
        out = jnp.float32(vals[-1])
        for gi in range(len(vals) - 2, -1, -1):
            out = jnp.where(group == gi, jnp.float32(vals[gi]), out)
        return out

    for h in range(n_heads):
        b0_ref[h] = neg_slope(h) * d_loc

    def scores(j):
        start = pl.multiple_of(j * t, t)
        return tuple(jnp.dot(k_ref[0, pl.ds(start, t), h * LANES:(h + 1) * LANES], qt_ref[h],
                             preferred_element_type=F32) for h in range(n_heads))

    def pv(j, ps):
        return [jnp.dot(vt_ref[idx // 2, j], ps[idx], preferred_element_type=F32) for idx in range(2 * n_heads)]

    def softmax_stage(ss, j, diag):
        ps, alphas = [], []
        for h in range(n_heads):
            if diag:
                bias = neg_slope(h) * jnp.abs(d_loc)
                off = None
            else:
                bias = b0_ref[h]
                off = neg_slope(h) * ((qi - j) * t).astype(F32)
            for m in range(2):
                idx = 2 * h + m
                a = ss[h][:, m * t:(m + 1) * t] + bias
                if diag:
                    a = jnp.where(chunk_mask, a, NEG_INF)
                m_old = m_ref[idx]
                a_max = jnp.max(a, axis=0, keepdims=True)
                m_new = jnp.maximum(m_old, a_max if diag else a_max + off)
                alpha = jnp.exp(m_old - m_new)
                p = jnp.exp(a - (m_new if diag else m_new - off))
                l_ref[idx] = alpha * l_ref[idx] + jnp.sum(p, axis=0, keepdims=True)
                m_ref[idx] = m_new
                ps.append(p.astype(BF16))
                alphas.append(alpha)
        return tuple(ps), alphas

    def body(j, carry):
        ss, ps_prev = carry
        prod = pv(jnp.maximum(j - 1, 0), ps_prev)
        ss_next = scores(j + 1)
        ps, alphas = softmax_stage(ss, j, False)
        for idx in range(2 * n_heads):
            acc_ref[idx] = alphas[idx] * (acc_ref[idx] + prod[idx])
        return ss_next, ps

    no_ps = tuple(jnp.zeros((t, t), BF16) for _ in range(2 * n_heads))
    ss, ps_prev = lax.fori_loop(0, qi, body, (scores(0), no_ps))
    prod = pv(jnp.maximum(qi - 1, 0), ps_prev)
    ps, alphas = softmax_stage(ss, qi, True)
    last = pv(qi, ps)
    for idx in range(2 * n_heads):
        acc_ref[idx] = alphas[idx] * (acc_ref[idx] + prod[idx]) + last[idx]

    lambda_init = scal_ref[0]
    lam = (jnp.exp(jnp.sum(lq1_ref[...] * lk1_ref[...], axis=-1, keepdims=True))
           - jnp.exp(jnp.sum(lq2_ref[...] * lk2_ref[...], axis=-1, keepdims=True)) + lambda_init)
    for h in range(n_heads):
        o_t = (acc_ref[2 * h] * (1.0 / l_ref[2 * h])
               - lam * (acc_ref[2 * h + 1] * (1.0 / l_ref[2 * h + 1])))
        o_ref[0, :, h * LANES:(h + 1) * LANES] = (_rms(o_t.T, g_ref[...]) * scal_ref[1]).astype(BF16)


def _da_attention(proj, scal, lq1, lk1, lq2, lk2, g, *, t=128, n_heads=DA_HEADS):
    b, s, _ = proj.shape
    nkb = s // t
    w = n_heads * LANES
    groups = DA_HEADS // n_heads
    q_off = (3 * SB_PAIRS * LANES) // w
    k_off = q_off + groups
    v_off = k_off + groups
    vec = lambda n: pl.BlockSpec((1, n), lambda bg, qi: (0, 0))
    return pl.pallas_call(
        functools.partial(_da_kernel, t=t, n_heads=n_heads,
                          head0_of_group=tuple(gi * n_heads for gi in range(groups))),
        grid=(b * groups, s // t),
        in_specs=[
            pl.BlockSpec(memory_space=pltpu.SMEM),
            vec(HEAD_DIM), vec(HEAD_DIM), vec(HEAD_DIM), vec(HEAD_DIM), vec(LANES),
            pl.BlockSpec((1, t, w), lambda bg, qi: (bg // groups, qi, q_off + bg % groups)),
            pl.BlockSpec((1, s, w), lambda bg, qi: (bg // groups, 0, k_off + bg % groups)),
            pl.BlockSpec((1, s, w), lambda bg, qi: (bg // groups, 0, v_off + bg % groups)),
        ],
        out_specs=pl.BlockSpec((1, t, w), lambda bg, qi: (bg // groups, qi, bg % groups)),
        out_shape=jax.ShapeDtypeStruct((b, s, DA_HEADS * LANES), BF16),
        scratch_shapes=[
            pltpu.VMEM((n_heads, nkb, LANES, t), BF16),
            pltpu.VMEM((n_heads, LANES, 2 * t), BF16),
            pltpu.VMEM((2 * n_heads, LANES, t), F32),
            pltpu.VMEM((2 * n_heads, 1, t), F32),
            pltpu.VMEM((2 * n_heads, 1, t), F32),
            pltpu.VMEM((n_heads, t, t), F32),
        ],
        compiler_params=pltpu.CompilerParams(
            dimension_semantics=("arbitrary", "arbitrary"), vmem_limit_bytes=VMEM_LIMIT_BYTES),
        name="da_attention",
    )(scal, lq1.reshape(1, -1), lk1.reshape(1, -1), lq2.reshape(1, -1), lk2.reshape(1, -1),
      g.reshape(1, -1), proj, proj, proj)


def _out_ffn_kernel(x_ref, sb_ref, da_ref, mod_ref, wout_ref, apost_ref, fpre_ref, wup_ref, cw_ref, cb_ref,
                    wdown_ref, fpost_ref, o_ref, carry_ref, ubuf_ref, act_ref, *, tm, fc, halo):
    i = pl.program_id(1)
    d_ff = wdown_ref.shape[0]

    @pl.when(i == 0)
    def _():
        carry_ref[...] = jnp.zeros_like(carry_ref)

    m = mod_ref[0]
    n_sb = sb_ref.shape[2]
    mixed = (jnp.dot(sb_ref[0], wout_ref[0:n_sb, :], preferred_element_type=F32)
             + jnp.dot(da_ref[0], wout_ref[n_sb:, :], preferred_element_type=F32))
    x1 = x_ref[0] + m[2:3] * _rms(mixed, apost_ref[...])
    h2 = (_rms(x1, fpre_ref[...]) * (1.0 + m[4:5]) + m[3:4]).astype(BF16)

    for ci in range(d_ff // fc):
        halves = []
        for part in range(2):
            col = part * d_ff + ci * fc
            u = jnp.dot(h2, wup_ref[:, col:col + fc], preferred_element_type=F32)
            ubuf_ref[0:halo, :] = carry_ref[:, col:col + fc]
            ubuf_ref[halo:halo + tm, :] = u
            carry_ref[:, col:col + fc] = u[tm - halo:tm]
            cw = cw_ref[:, col:col + fc]
            y = cb_ref[:, col:col + fc] + cw[CONV_WIDTH - 1:CONV_WIDTH] * u
            for tap in range(CONV_WIDTH - 1):
                back = CONV_WIDTH - 1 - tap
                y = y + cw[tap:tap + 1] * ubuf_ref[pl.ds(halo - back, tm), :]
            halves.append(y)
        gate, val = halves
        act_ref[:, ci * fc:(ci + 1) * fc] = (gate * jax.nn.sigmoid(gate) * val).astype(BF16)
    f = jnp.dot(act_ref[...], wdown_ref[...], preferred_element_type=F32)
    o_ref[0] = x1 + m[5:6] * _rms(f, fpost_ref[...])


def _out_ffn(x, sb, da, mod_l, w_out, apost, fpre, w_up, conv_w, conv_b, w_down, fpost, *, tm=256, fc=256):
    b, s, d = x.shape
    d_ff = w_down.shape[0]
    halo = 8
    const = lambda shape: pl.BlockSpec(shape, lambda bi, i: (0,) * len(shape), pipeline_mode=pl.Buffered(1))
    row = lambda n: pl.BlockSpec((1, tm, n), lambda bi, i: (bi, i, 0))
    return pl.pallas_call(
        functools.partial(_out_ffn_kernel, tm=tm, fc=fc, halo=halo),
        grid=(b, s // tm),
        in_specs=[
            row(d), row(sb.shape[2]), row(da.shape[2]),
            pl.BlockSpec((1, N_MOD, d), lambda bi, i: (bi, 0, 0)),
            const(w_out.shape), const((1, d)), const((1, d)),
            const(w_up.shape), const(conv_w.shape), const((1, 2 * d_ff)),
            const(w_down.shape), const((1, d)),
        ],
        out_specs=row(d),
        out_shape=jax.ShapeDtypeStruct((b, s, d), F32),
        scratch_shapes=[
            pltpu.VMEM((halo, 2 * d_ff), F32),
            pltpu.VMEM((halo + tm, fc), F32),
            pltpu.VMEM((tm, d_ff), BF16),
        ],
        compiler_params=pltpu.CompilerParams(
            dimension_semantics=("arbitrary", "arbitrary"), vmem_limit_bytes=VMEM_LIMIT_BYTES),
        name="out_ffn",
    )(x, sb, da, mod_l, w_out, apost.reshape(1, d), fpre.reshape(1, d), w_up, conv_w,
      conv_b.reshape(1, 2 * d_ff), w_down, fpost.reshape(1, d))


def kernel(x, c, ada_w, ada_b, attn_pre_g, attn_post_g, w_in, w_out, lambda_q1, lambda_k1, lambda_q2, lambda_k2,
           da_subln_g, ffn_pre_g, ffn_post_g, w_up, conv_w, conv_b, w_down):
    depth = ada_w.shape[0]
    b, s, d = x.shape
    mod = _ada_mod(c, ada_w, ada_b).reshape(depth, b, N_MOD, d)
    w_in, w_out, w_up, w_down = (w.astype(BF16) for w in (w_in, w_out, w_up, w_down))
    for l in range(depth):
        proj = _in_proj(x, mod[l], attn_pre_g[l], w_in[l])
        sb = _sb_attention(proj)
        lambda_init = 0.8 - 0.6 * math.exp(-0.3 * l)
        scal = jnp.array([lambda_init, 1.0 - lambda_init], F32)
        da = _da_attention(proj, scal, lambda_q1[l], lambda_k1[l], lambda_q2[l], lambda_k2[l], da_subln_g[l])
        x = _out_ffn(x, sb, da, mod[l], w_out[l], attn_post_g[l], ffn_pre_g[l], w_up[l], conv_w[l], conv_b[l],
                     w_down[l], ffn_post_g[l])
    return x
```

```python
import functools
import math

import jax
import jax.numpy as jnp
from jax import lax
from jax.experimental import pallas as pl
from jax.experimental.pallas import tpu as pltpu

F32 = jnp.float32
BF16 = jnp.bfloat16

CHUNK = 64
SB_HEADS = 8
HEAD_DIM = 64
DA_HEADS = 4
N_MOD = 6
CONV_WIDTH = 3
EPS = 1e-6
NEG_INF = -1e30

LANES = 128
SB_PAIRS = SB_HEADS // 2
VMEM_LIMIT_BYTES = 56 * 1024 * 1024

SB_SKIP_THRESHOLD = 110.0
SB_SKEW = 1


def _rms(x, g):
    return x * lax.rsqrt(jnp.mean(x * x, axis=-1, keepdims=True) + EPS) * g


def _ada_kernel(c_ref, w_ref, b_ref, o_ref):
    c = c_ref[...]
    c_act = (c * jax.nn.sigmoid(c)).astype(BF16)
    o_ref[0] = jnp.dot(c_act, w_ref[0].astype(BF16), preferred_element_type=F32) + b_ref[0]


def _ada_mod(c, ada_w, ada_b, *, tn=1536):
    depth, d, n = ada_w.shape
    b = c.shape[0]
    return pl.pallas_call(
        _ada_kernel,
        grid=(depth, n // tn),
        in_specs=[
            pl.BlockSpec((b, d), lambda l, j: (0, 0)),
            pl.BlockSpec((1, d, tn), lambda l, j: (l, 0, j)),
            pl.BlockSpec((1, 1, tn), lambda l, j: (l, 0, j)),
        ],
        out_specs=pl.BlockSpec((1, b, tn), lambda l, j: (l, 0, j)),
        out_shape=jax.ShapeDtypeStruct((depth, b, n), F32),
        compiler_params=pltpu.CompilerParams(
            dimension_semantics=("arbitrary", "arbitrary"), vmem_limit_bytes=VMEM_LIMIT_BYTES),
        name="ada_mod",
    )(c, ada_w, ada_b.reshape(depth, 1, n))


def _in_proj_kernel(x_ref, mod_ref, g_ref, w_ref, o_ref, *, n_chunk):
    m = mod_ref[0]
    h = (_rms(x_ref[0], g_ref[...]) * (1.0 + m[1:2]) + m[0:1]).astype(BF16)
    n = w_ref.shape[1]
    for n0 in range(0, n, n_chunk):
        o_ref[0, :, n0:n0 + n_chunk] = jnp.dot(
            h, w_ref[:, n0:n0 + n_chunk], preferred_element_type=F32).astype(BF16)


def _in_proj(x, mod_l, g, w, *, tm=512, n_chunk=512):
    b, s, d = x.shape
    n = w.shape[1]
    return pl.pallas_call(
        functools.partial(_in_proj_kernel, n_chunk=n_chunk),
        grid=(b, s // tm),
        in_specs=[
            pl.BlockSpec((1, tm, d), lambda bi, i: (bi, i, 0)),
            pl.BlockSpec((1, N_MOD, d), lambda bi, i: (bi, 0, 0)),
            pl.BlockSpec((1, d), lambda bi, i: (0, 0)),
            pl.BlockSpec((d, n), lambda bi, i: (0, 0)),
        ],
        out_specs=pl.BlockSpec((1, tm, n), lambda bi, i: (bi, i, 0)),
        out_shape=jax.ShapeDtypeStruct((b, s, n), BF16),
        compiler_params=pltpu.CompilerParams(
            dimension_semantics=("arbitrary", "arbitrary"), vmem_limit_bytes=VMEM_LIMIT_BYTES),
        name="in_proj",
    )(x, mod_l, g.reshape(1, d), w)


def _split_halves_t(blk):
    bt = blk.astype(F32).T
    row = lax.broadcasted_iota(jnp.int32, bt.shape, 0)
    return jnp.where(row < HEAD_DIM, bt, 0.0), jnp.where(row >= HEAD_DIM, bt, 0.0)


def _masked_qt(q_blk):
    lo, hi = _split_halves_t(q_blk * jnp.asarray(HEAD_DIM ** -0.5, BF16))
    return jnp.concatenate([lo, hi], axis=1).astype(BF16)


def _sb_kernel(q_ref, k_ref, v_ref, o_ref, vt_ref, qt_ref, acc_ref, c_ref, zz_ref, ws_ref, *, t, n_pairs):
    qi = pl.program_id(1)
    nkb = vt_ref.shape[1]

    @pl.when(qi == 0)
    def _():
        def body(j, carry):
            start = pl.multiple_of(j * t, t)
            for p in range(n_pairs):
                lo, hi = _split_halves_t(v_ref[0, pl.ds(start, t), p * LANES:(p + 1) * LANES])
                vt_ref[p, j] = jnp.concatenate([lo, hi], axis=1).astype(BF16)
            return carry

        lax.fori_loop(0, nkb, body, 0)

    for p in range(n_pairs):
        qt_ref[p] = _masked_qt(q_ref[0, :, p * LANES:(p + 1) * LANES])
    acc_ref[...] = jnp.zeros_like(acc_ref)
    c_ref[...] = jnp.zeros_like(c_ref)

    s_loc = lax.broadcasted_iota(jnp.int32, (t, t), 0)
    q_loc = lax.broadcasted_iota(jnp.int32, (t, t), 1)
    tri = (q_loc >= s_loc).astype(BF16)
    q_both = lax.broadcasted_iota(jnp.int32, (t, 2 * t), 1)
    causal = lax.broadcasted_iota(jnp.int32, (t, 2 * t), 0) < jnp.where(q_both >= t, q_both - t, q_both)

    def scores_into(slot, j):
        start = pl.multiple_of(j * t, t)
        for p in range(n_pairs):
            zz_ref[slot, p] = jnp.dot(k_ref[0, pl.ds(start, t), p * LANES:(p + 1) * LANES], qt_ref[p],
                                      preferred_element_type=F32)

    def stage(j, slot, diag):
        other = 1 - slot
        c_in = [c_ref[p] for p in range(n_pairs)]
        smallest = jnp.min(functools.reduce(jnp.minimum, c_in))
        suffix, prod = [None] * n_pairs, None
        for step in range(n_pairs + SB_SKEW):
            if step < n_pairs:
                z = zz_ref[slot, step]
                sp = jnp.where(z > 80.0, z, jnp.log(1.0 + jnp.exp(z)))
                if diag:
                    sp = jnp.where(causal, sp, 0.0)
                suffix[step] = jnp.dot(tri, sp.astype(BF16), preferred_element_type=F32)
            if step == 0:
                prod = [jnp.dot(vt_ref[p, jnp.minimum(j + 1, nkb - 1)], ws_ref[other, p],
                                preferred_element_type=F32) for p in range(n_pairs)]
                scores_into(other, jnp.maximum(j - 1, 0))
            i = step - SB_SKEW
            if i >= 0:
                cs = suffix[i] + c_in[i]
                w = jnp.exp(zz_ref[slot, i] - cs)
                if diag:
                    w = jnp.where(causal, w, 0.0)
                w = w.astype(BF16)
                ws_ref[slot, i, 0:t, :] = w[:, 0:t]
                ws_ref[slot, i, t:2 * t, :] = w[:, t:2 * t]
                c_ref[i] = cs[0:1, :]
        for p in range(n_pairs):
            acc_ref[p] += prod[p]
        return smallest

    def stage_by_parity(j, diag):
        return lax.cond(j % 2 == 0, lambda: stage(j, 0, diag), lambda: stage(j, 1, diag))

    ws_ref[...] = jnp.zeros_like(ws_ref)
    for slot in range(2):
        @pl.when(qi % 2 == slot)
        def _():
            scores_into(slot, qi)

    smallest = stage_by_parity(qi, True)

    def cond(state):
        j, smallest = state
        return jnp.logical_and(j >= 0, smallest < SB_SKIP_THRESHOLD)

    j_end, _ = lax.while_loop(cond, lambda state: (state[0] - 1, stage_by_parity(state[0], False)),
                              (qi - 1, smallest))
    last = j_end + 1
    for slot in range(2):
        @pl.when(last % 2 == slot)
        def _():
            for p in range(n_pairs):
                acc_ref[p] += jnp.dot(vt_ref[p, last], ws_ref[slot, p], preferred_element_type=F32)

    for p in range(n_pairs):
        o_ref[0, :, p * LANES:(p + 1) * LANES] = acc_ref[p].T.astype(BF16)


def _sb_attention(proj, *, t=128, n_pairs=SB_PAIRS):
    b, s, _ = proj.shape
    nkb = s // t
    w = n_pairs * LANES
    groups = SB_PAIRS // n_pairs
    return pl.pallas_call(
        functools.partial(_sb_kernel, t=t, n_pairs=n_pairs),
        grid=(b * groups, s // t),
        in_specs=[
            pl.BlockSpec((1, t, w), lambda bg, qi: (bg // groups, qi, bg % groups)),
            pl.BlockSpec((1, s, w), lambda bg, qi: (bg // groups, 0, groups + bg % groups)),
            pl.BlockSpec((1, s, w), lambda bg, qi: (bg // groups, 0, 2 * groups + bg % groups)),
        ],
        out_specs=pl.BlockSpec((1, t, w), lambda bg, qi: (bg // groups, qi, bg % groups)),
        out_shape=jax.ShapeDtypeStruct((b, s, SB_PAIRS * LANES), BF16),
        scratch_shapes=[
            pltpu.VMEM((n_pairs, nkb, LANES, 2 * t), BF16),
            pltpu.VMEM((n_pairs, LANES, 2 * t), BF16),
            pltpu.VMEM((n_pairs, LANES, t), F32),
            pltpu.VMEM((n_pairs, 1, 2 * t), F32),
            pltpu.VMEM((2, n_pairs, t, 2 * t), F32),
            pltpu.VMEM((2, n_pairs, 2 * t, t), BF16),
        ],
        compiler_params=pltpu.CompilerParams(
            dimension_semantics=("arbitrary", "arbitrary"), vmem_limit_bytes=VMEM_LIMIT_BYTES),
        name="sb_attention",
    )(proj, proj, proj)


def _da_kernel(scal_ref, lq1_ref, lk1_ref, lq2_ref, lk2_ref, g_ref, q_ref, k_ref, v_ref, o_ref,
               vt_ref, qt_ref, acc_ref, m_ref, l_ref, b0_ref, ss_ref, p_ref, *, t, n_heads, head0_of_group):
    qi = pl.program_id(1)
    nkb = vt_ref.shape[1]
    group = pl.program_id(0) % len(head0_of_group)

    @pl.when(qi == 0)
    def _():
        def body(j, carry):
            start = pl.multiple_of(j * t, t)
            for h in range(n_heads):
                vt_ref[h, j] = v_ref[0, pl.ds(start, t), h * LANES:(h + 1) * LANES].astype(F32).T.astype(BF16)
            return carry

        lax.fori_loop(0, nkb, body, 0)

    for h in range(n_heads):
        qt_ref[h] = _masked_qt(q_ref[0, :, h * LANES:(h + 1) * LANES])

    s_loc = lax.broadcasted_iota(jnp.int32, (t, 2 * t), 0)
    q_both = lax.broadcasted_iota(jnp.int32, (t, 2 * t), 1)
    q_loc = jnp.where(q_both >= t, q_both - t, q_both)
    d_loc = (q_loc - s_loc).astype(F32)
    chunk_mask = (s_loc // CHUNK) <= (q_loc // CHUNK)

    def neg_slope(h):
        vals = [-(2.0 ** (-8.0 * (h0 + h + 1) / DA_HEADS)) for h0 in head0_of_group]
        out = jnp.float32(vals[-1])
        for gi in range(len(vals) - 2, -1, -1):
            out = jnp.where(group == gi, jnp.float32(vals[gi]), out)
        return out

    @pl.when(qi == 0)
    def _():
        for h in range(n_heads):
            b0_ref[h] = neg_slope(h) * d_loc

    def scores_into(slot, j):
        start = pl.multiple_of(j * t, t)
        for h in range(n_heads):
            ss_ref[slot, h] = jnp.dot(k_ref[0, pl.ds(start, t), h * LANES:(h + 1) * LANES], qt_ref[h],
                                      preferred_element_type=F32)

    def stage(j, slot, diag):
        other = 1 - slot
        prod = [jnp.dot(vt_ref[h, jnp.maximum(j - 1, 0)], p_ref[other, h], preferred_element_type=F32)
                for h in range(n_heads)]
        if not diag:
            scores_into(other, j + 1)
        for h in range(n_heads):
            if diag:
                a = jnp.where(chunk_mask, ss_ref[slot, h] + neg_slope(h) * jnp.abs(d_loc), NEG_INF)
                off = None
            else:
                a = ss_ref[slot, h] + b0_ref[h]
                off = neg_slope(h) * ((qi - j) * t).astype(F32)
            m_old = m_ref[h]
            a_max = jnp.max(a, axis=0, keepdims=True)
            m_new = jnp.maximum(m_old, a_max if diag else a_max + off)
            alpha = jnp.exp(m_old - m_new)
            p = jnp.exp(a - (m_new if diag else m_new - off))
            l_ref[h] = alpha * l_ref[h] + jnp.sum(p, axis=0, keepdims=True)
            m_ref[h] = m_new
            p_ref[slot, h] = p.astype(BF16)
            acc_ref[h] = alpha * (acc_ref[h] + prod[h])
        if diag:
            for h in range(n_heads):
                acc_ref[h] += jnp.dot(vt_ref[h, j], p_ref[slot, h], preferred_element_type=F32)

    def stage_by_parity(j, diag):
        for slot in range(2):
            @pl.when(j % 2 == slot)
            def _():
                stage(j, slot, diag)

    acc_ref[...] = jnp.zeros_like(acc_ref)
    l_ref[...] = jnp.zeros_like(l_ref)
    m_ref[...] = jnp.full_like(m_ref, NEG_INF)
    p_ref[1] = jnp.zeros_like(p_ref[1])
    scores_into(0, 0)

    def body(j, carry):
        stage_by_parity(j, False)
        return carry

    lax.fori_loop(0, qi, body, 0)
    stage_by_parity(qi, True)

    lambda_init = scal_ref[0]
    lam = (jnp.exp(jnp.sum(lq1_ref[...] * lk1_ref[...], axis=-1, keepdims=True))
           - jnp.exp(jnp.sum(lq2_ref[...] * lk2_ref[...], axis=-1, keepdims=True)) + lambda_init)
    for h in range(n_heads):
        both = acc_ref[h] * (1.0 / l_ref[h])
        o_t = both[:, 0:t] - lam * both[:, t:2 * t]
        o_ref[0, :, h * LANES:(h + 1) * LANES] = (_rms(o_t.T, g_ref[...]) * scal_ref[1]).astype(BF16)


def _da_attention(proj, scal, lq1, lk1, lq2, lk2, g, *, t=128, n_heads=DA_HEADS):
    b, s, _ = proj.shape
    nkb = s // t
    w = n_heads * LANES
    groups = DA_HEADS // n_heads
    q_off = (3 * SB_PAIRS * LANES) // w
    k_off = q_off + groups
    v_off = k_off + groups
    vec = lambda n: pl.BlockSpec((1, n), lambda bg, qi: (0, 0))
    return pl.pallas_call(
        functools.partial(_da_kernel, t=t, n_heads=n_heads,
                          head0_of_group=tuple(gi * n_heads for gi in range(groups))),
        grid=(b * groups, s // t),
        in_specs=[
            pl.BlockSpec(memory_space=pltpu.SMEM),
            vec(HEAD_DIM), vec(HEAD_DIM), vec(HEAD_DIM), vec(HEAD_DIM), vec(LANES),
            pl.BlockSpec((1, t, w), lambda bg, qi: (bg // groups, qi, q_off + bg % groups)),
            pl.BlockSpec((1, s, w), lambda bg, qi: (bg // groups, 0, k_off + bg % groups)),
            pl.BlockSpec((1, s, w), lambda bg, qi: (bg // groups, 0, v_off + bg % groups)),
        ],
        out_specs=pl.BlockSpec((1, t, w), lambda bg, qi: (bg // groups, qi, bg % groups)),
        out_shape=jax.ShapeDtypeStruct((b, s, DA_HEADS * LANES), BF16),
        scratch_shapes=[
            pltpu.VMEM((n_heads, nkb, LANES, t), BF16),
            pltpu.VMEM((n_heads, LANES, 2 * t), BF16),
            pltpu.VMEM((n_heads, LANES, 2 * t), F32),
            pltpu.VMEM((n_heads, 1, 2 * t), F32),
            pltpu.VMEM((n_heads, 1, 2 * t), F32),
            pltpu.VMEM((n_heads, t, 2 * t), F32),
            pltpu.VMEM((2, n_heads, t, 2 * t), F32),
            pltpu.VMEM((2, n_heads, t, 2 * t), BF16),
        ],
        compiler_params=pltpu.CompilerParams(
            dimension_semantics=("arbitrary", "arbitrary"), vmem_limit_bytes=VMEM_LIMIT_BYTES),
        name="da_attention",
    )(scal, lq1.reshape(1, -1), lk1.reshape(1, -1), lq2.reshape(1, -1), lk2.reshape(1, -1),
      g.reshape(1, -1), proj, proj, proj)


def _out_ffn_kernel(x_ref, sb_ref, da_ref, mod_ref, wout_ref, apost_ref, fpre_ref, wup_ref, cw_ref, cb_ref,
                    wdown_ref, fpost_ref, o_ref, carry_ref, ubuf_ref, act_ref, *, tm, fc, halo):
    i = pl.program_id(1)
    d_ff = wdown_ref.shape[0]

    @pl.when(i == 0)
    def _():
        carry_ref[...] = jnp.zeros_like(carry_ref)

    m = mod_ref[0]
    n_sb = sb_ref.shape[2]
    mixed = (jnp.dot(sb_ref[0], wout_ref[0:n_sb, :], preferred_element_type=F32)
             + jnp.dot(da_ref[0], wout_ref[n_sb:, :], preferred_element_type=F32))
    x1 = x_ref[0] + m[2:3] * _rms(mixed, apost_ref[...])
    h2 = (_rms(x1, fpre_ref[...]) * (1.0 + m[4:5]) + m[3:4]).astype(BF16)

    for ci in range(d_ff // fc):
        halves = []
        for part in range(2):
            col = part * d_ff + ci * fc
            u = jnp.dot(h2, wup_ref[:, col:col + fc], preferred_element_type=F32)
            ubuf_ref[0:halo, :] = carry_ref[:, col:col + fc]
            ubuf_ref[halo:halo + tm, :] = u
            carry_ref[:, col:col + fc] = u[tm - halo:tm]
            cw = cw_ref[:, col:col + fc]
            y = cb_ref[:, col:col + fc] + cw[CONV_WIDTH - 1:CONV_WIDTH] * u
            for tap in range(CONV_WIDTH - 1):
                back = CONV_WIDTH - 1 - tap
                y = y + cw[tap:tap + 1] * ubuf_ref[pl.ds(halo - back, tm), :]
            halves.append(y)
        gate, val = halves
        act_ref[:, ci * fc:(ci + 1) * fc] = (gate * jax.nn.sigmoid(gate) * val).astype(BF16)
    f = jnp.dot(act_ref[...], wdown_ref[...], preferred_element_type=F32)
    o_ref[0] = x1 + m[5:6] * _rms(f, fpost_ref[...])


def _out_ffn(x, sb, da, mod_l, w_out, apost, fpre, w_up, conv_w, conv_b, w_down, fpost, *, tm=256, fc=256):
    b, s, d = x.shape
    d_ff = w_down.shape[0]
    halo = 8
    const = lambda shape: pl.BlockSpec(shape, lambda bi, i: (0,) * len(shape), pipeline_mode=pl.Buffered(1))
    row = lambda n: pl.BlockSpec((1, tm, n), lambda bi, i: (bi, i, 0))
    return pl.pallas_call(
        functools.partial(_out_ffn_kernel, tm=tm, fc=fc, halo=halo),
        grid=(b, s // tm),
        in_specs=[
            row(d), row(sb.shape[2]), row(da.shape[2]),
            pl.BlockSpec((1, N_MOD, d), lambda bi, i: (bi, 0, 0)),
            const(w_out.shape), const((1, d)), const((1, d)),
            const(w_up.shape), const(conv_w.shape), const((1, 2 * d_ff)),
            const(w_down.shape), const((1, d)),
        ],
        out_specs=row(d),
        out_shape=jax.ShapeDtypeStruct((b, s, d), F32),
        scratch_shapes=[
            pltpu.VMEM((halo, 2 * d_ff), F32),
            pltpu.VMEM((halo + tm, fc), F32),
            pltpu.VMEM((tm, d_ff), BF16),
        ],
        compiler_params=pltpu.CompilerParams(
            dimension_semantics=("arbitrary", "arbitrary"), vmem_limit_bytes=VMEM_LIMIT_BYTES),
        name="out_ffn",
    )(x, sb, da, mod_l, w_out, apost.reshape(1, d), fpre.reshape(1, d), w_up, conv_w,
      conv_b.reshape(1, 2 * d_ff), w_down, fpost.reshape(1, d))


def kernel(x, c, ada_w, ada_b, attn_pre_g, attn_post_g, w_in, w_out, lambda_q1, lambda_k1, lambda_q2, lambda_k2,
           da_subln_g, ffn_pre_g, ffn_post_g, w_up, conv_w, conv_b, w_down):
    depth = ada_w.shape[0]
    b, s, d = x.shape
    mod = _ada_mod(c, ada_w, ada_b).reshape(depth, b, N_MOD, d)
    w_in, w_out, w_up, w_down = (w.astype(BF16) for w in (w_in, w_out, w_up, w_down))
    for l in range(depth):
        proj = _in_proj(x, mod[l], attn_pre_g[l], w_in[l])
        sb = _sb_attention(proj)
        lambda_init = 0.8 - 0.6 * math.exp(-0.3 * l)
        scal = jnp.array([lambda_init, 1.0 - lambda_init], F32)
        da = _da_attention(proj, scal, lambda_q1[l], lambda_k1[l], lambda_q2[l], lambda_k2[l], da_subln_g[l])
        x = _out_ffn(x, sb, da, mod[l], w_out[l], attn_post_g[l], ffn_pre_g[l], w_up[l], conv_w[l], conv_b[l],
                     w_down[l], ffn_post_g[l])
    return x
```

```python
import functools
import math

import jax
import jax.numpy as jnp
from jax import lax
from jax.experimental import pallas as pl
from jax.experimental.pallas import tpu as pltpu

F32 = jnp.float32
BF16 = jnp.bfloat16

CHUNK = 64
SB_HEADS = 8
HEAD_DIM = 64
DA_HEADS = 4
N_MOD = 6
CONV_WIDTH = 3
EPS = 1e-6
NEG_INF = -1e30

LANES = 128
SB_PAIRS = SB_HEADS // 2
VMEM_LIMIT_BYTES = 56 * 1024 * 1024

SB_SKIP_THRESHOLD = 110.0
SB_SKEW = 1


def _rms(x, g):
    return x * lax.rsqrt(jnp.mean(x * x, axis=-1, keepdims=True) + EPS) * g


def _ada_kernel(c_ref, w_ref, b_ref, o_ref):
    c = c_ref[...]
    c_act = (c * jax.nn.sigmoid(c)).astype(BF16)
    o_ref[0] = jnp.dot(c_act, w_ref[0].astype(BF16), preferred_element_type=F32) + b_ref[0]


def _ada_mod(c, ada_w, ada_b, *, tn=1536):
    depth, d, n = ada_w.shape
    b = c.shape[0]
    return pl.pallas_call(
        _ada_kernel,
        grid=(depth, n // tn),
        in_specs=[
            pl.BlockSpec((b, d), lambda l, j: (0, 0)),
            pl.BlockSpec((1, d, tn), lambda l, j: (l, 0, j)),
            pl.BlockSpec((1, 1, tn), lambda l, j: (l, 0, j)),
        ],
        out_specs=pl.BlockSpec((1, b, tn), lambda l, j: (l, 0, j)),
        out_shape=jax.ShapeDtypeStruct((depth, b, n), F32),
        compiler_params=pltpu.CompilerParams(
            dimension_semantics=("arbitrary", "arbitrary"), vmem_limit_bytes=VMEM_LIMIT_BYTES),
        name="ada_mod",
    )(c, ada_w, ada_b.reshape(depth, 1, n))


def _in_proj_kernel(x_ref, mod_ref, g_ref, w_ref, o_ref, *, n_chunk):
    m = mod_ref[0]
    h = (_rms(x_ref[0], g_ref[...]) * (1.0 + m[1:2]) + m[0:1]).astype(BF16)
    n = w_ref.shape[1]
    for n0 in range(0, n, n_chunk):
        o_ref[0, :, n0:n0 + n_chunk] = jnp.dot(
            h, w_ref[:, n0:n0 + n_chunk], preferred_element_type=F32).astype(BF16)


def _in_proj(x, mod_l, g, w, *, tm=512, n_chunk=512):
    b, s, d = x.shape
    n = w.shape[1]
    return pl.pallas_call(
        functools.partial(_in_proj_kernel, n_chunk=n_chunk),
        grid=(b, s // tm),
        in_specs=[
            pl.BlockSpec((1, tm, d), lambda bi, i: (bi, i, 0)),
            pl.BlockSpec((1, N_MOD, d), lambda bi, i: (bi, 0, 0)),
            pl.BlockSpec((1, d), lambda bi, i: (0, 0)),
            pl.BlockSpec((d, n), lambda bi, i: (0, 0)),
        ],
        out_specs=pl.BlockSpec((1, tm, n), lambda bi, i: (bi, i, 0)),
        out_shape=jax.ShapeDtypeStruct((b, s, n), BF16),
        compiler_params=pltpu.CompilerParams(
            dimension_semantics=("arbitrary", "arbitrary"), vmem_limit_bytes=VMEM_LIMIT_BYTES),
        name="in_proj",
    )(x, mod_l, g.reshape(1, d), w)


def _split_halves_t(blk):
    bt = blk.astype(F32).T
    row = lax.broadcasted_iota(jnp.int32, bt.shape, 0)
    return jnp.where(row < HEAD_DIM, bt, 0.0), jnp.where(row >= HEAD_DIM, bt, 0.0)


def _masked_qt(q_blk):
    lo, hi = _split_halves_t(q_blk * jnp.asarray(HEAD_DIM ** -0.5, BF16))
    return jnp.concatenate([lo, hi], axis=1).astype(BF16)


def _sb_kernel(q_ref, k_ref, v_ref, o_ref, vt_ref, qt_ref, acc_ref, c_ref, zz_ref, ws_ref, *, t, n_pairs):
    qi = pl.program_id(1)
    nkb = vt_ref.shape[1]

    @pl.when(qi == 0)
    def _():
        def body(j, carry):
            start = pl.multiple_of(j * t, t)
            for p in range(n_pairs):
                lo, hi = _split_halves_t(v_ref[0, pl.ds(start, t), p * LANES:(p + 1) * LANES])
                vt_ref[p, j] = jnp.concatenate([lo, hi], axis=1).astype(BF16)
            return carry

        lax.fori_loop(0, nkb, body, 0)

    for p in range(n_pairs):
        qt_ref[p] = _masked_qt(q_ref[0, :, p * LANES:(p + 1) * LANES])
    acc_ref[...] = jnp.zeros_like(acc_ref)
    c_ref[...] = jnp.zeros_like(c_ref)

    s_loc = lax.broadcasted_iota(jnp.int32, (t, t), 0)
    q_loc = lax.broadcasted_iota(jnp.int32, (t, t), 1)
    tri = (q_loc >= s_loc).astype(BF16)
    q_both = lax.broadcasted_iota(jnp.int32, (t, 2 * t), 1)
    causal = lax.broadcasted_iota(jnp.int32, (t, 2 * t), 0) < jnp.where(q_both >= t, q_both - t, q_both)

    def scores_into(slot, j):
        start = pl.multiple_of(j * t, t)
        for p in range(n_pairs):
            zz_ref[slot, p] = jnp.dot(k_ref[0, pl.ds(start, t), p * LANES:(p + 1) * LANES], qt_ref[p],
                                      preferred_element_type=F32)

    def stage(j, slot, diag):
        other = 1 - slot
        c_in = [c_ref[p] for p in range(n_pairs)]
        smallest = jnp.min(functools.reduce(jnp.minimum, c_in))
        suffix, prod = [None] * n_pairs, None
        for step in range(n_pairs + SB_SKEW):
            if step < n_pairs:
                z = zz_ref[slot, step]
                sp = jnp.where(z > 80.0, z, jnp.log(1.0 + jnp.exp(z)))
                if diag:
                    sp = jnp.where(causal, sp, 0.0)
                suffix[step] = jnp.dot(tri, sp.astype(BF16), preferred_element_type=F32)
            if step == 0:
                prod = [jnp.dot(vt_ref[p, jnp.minimum(j + 1, nkb - 1)], ws_ref[other, p],
                                preferred_element_type=F32) for p in range(n_pairs)]
                scores_into(other, jnp.maximum(j - 1, 0))
            i = step - SB_SKEW
            if i >= 0:
                cs = suffix[i] + c_in[i]
                w = jnp.exp(zz_ref[slot, i] - cs)
                if diag:
                    w = jnp.where(causal, w, 0.0)
                w = w.astype(BF16)
                ws_ref[slot, i, 0:t, :] = w[:, 0:t]
                ws_ref[slot, i, t:2 * t, :] = w[:, t:2 * t]
                c_ref[i] = cs[0:1, :]
        for p in range(n_pairs):
            acc_ref[p] += prod[p]
        return smallest

    ws_ref[...] = jnp.zeros_like(ws_ref)

    def first_even():
        scores_into(0, qi)
        return stage(qi, 0, True)

    def first_odd():
        scores_into(1, qi)
        stage(qi, 1, True)
        return stage(qi - 1, 0, False)

    qi_even = qi % 2 == 0
    smallest = lax.cond(qi_even, first_even, first_odd)

    def cond(state):
        j, smallest = state
        return jnp.logical_and(j >= 1, smallest < SB_SKIP_THRESHOLD)

    def body(state):
        j = state[0]
        stage(j, 1, False)
        return j - 2, stage(j - 1, 0, False)

    j_end, _ = lax.while_loop(cond, body, (jnp.where(qi_even, qi - 1, qi - 2), smallest))
    last = j_end + 1
    for p in range(n_pairs):
        acc_ref[p] += jnp.dot(vt_ref[p, last], ws_ref[0, p], preferred_element_type=F32)

    for p in range(n_pairs):
        o_ref[0, :, p * LANES:(p + 1) * LANES] = acc_ref[p].T.astype(BF16)


def _sb_attention(proj, *, t=128, n_pairs=SB_PAIRS):
    b, s, _ = proj.shape
    nkb = s // t
    w = n_pairs * LANES
    groups = SB_PAIRS // n_pairs
    return pl.pallas_call(
        functools.partial(_sb_kernel, t=t, n_pairs=n_pairs),
        grid=(b * groups, s // t),
        in_specs=[
            pl.BlockSpec((1, t, w), lambda bg, qi: (bg // groups, qi, bg % groups)),
            pl.BlockSpec((1, s, w), lambda bg, qi: (bg // groups, 0, groups + bg % groups)),
            pl.BlockSpec((1, s, w), lambda bg, qi: (bg // groups, 0, 2 * groups + bg % groups)),
        ],
        out_specs=pl.BlockSpec((1, t, w), lambda bg, qi: (bg // groups, qi, bg % groups)),
        out_shape=jax.ShapeDtypeStruct((b, s, SB_PAIRS * LANES), BF16),
        scratch_shapes=[
            pltpu.VMEM((n_pairs, nkb, LANES, 2 * t), BF16),
            pltpu.VMEM((n_pairs, LANES, 2 * t), BF16),
            pltpu.VMEM((n_pairs, LANES, t), F32),
            pltpu.VMEM((n_pairs, 1, 2 * t), F32),
            pltpu.VMEM((2, n_pairs, t, 2 * t), F32),
            pltpu.VMEM((2, n_pairs, 2 * t, t), BF16),
        ],
        compiler_params=pltpu.CompilerParams(
            dimension_semantics=("arbitrary", "arbitrary"), vmem_limit_bytes=VMEM_LIMIT_BYTES),
        name="sb_attention",
    )(proj, proj, proj)


def _da_kernel(scal_ref, lq1_ref, lk1_ref, lq2_ref, lk2_ref, g_ref, q_ref, k_ref, v_ref, o_ref,
               vt_ref, qt_ref, acc_ref, m_ref, l_ref, kaug_ref, ss_ref, p_ref, *, t, n_heads, head0_of_group):
    qi = pl.program_id(1)
    nkb = vt_ref.shape[1]
    group = pl.program_id(0) % len(head0_of_group)

    def neg_slope(h):
        vals = [-(2.0 ** (-8.0 * (h0 + h + 1) / DA_HEADS)) for h0 in head0_of_group]
        out = jnp.float32(vals[-1])
        for gi in range(len(vals) - 2, -1, -1):
            out = jnp.where(group == gi, jnp.float32(vals[gi]), out)
        return out

    @pl.when(qi == 0)
    def _():
        lane = lax.broadcasted_iota(jnp.int32, (t, LANES), 1)
        s_row = lax.broadcasted_iota(jnp.int32, (t, LANES), 0).astype(F32)
        ones_rows = (lax.broadcasted_iota(jnp.int32, (LANES, 2 * t), 0) < 2).astype(BF16)

        def body(j, carry):
            start = pl.multiple_of(j * t, t)
            base = (j * t).astype(F32)
            for h in range(n_heads):
                cols = slice(h * LANES, (h + 1) * LANES)
                vt_ref[h, j] = v_ref[0, pl.ds(start, t), cols].astype(F32).T.astype(BF16)
                slope = -neg_slope(h)
                extra = jnp.where(lane == 0, slope * base, jnp.where(lane == 1, slope * s_row, 0.0))
                kaug_ref[h, j, :, 0:LANES] = k_ref[0, pl.ds(start, t), cols]
                kaug_ref[h, j, :, LANES:2 * LANES] = extra.astype(BF16)
            return carry

        lax.fori_loop(0, nkb, body, 0)
        for h in range(n_heads):
            qt_ref[h, LANES:2 * LANES, :] = ones_rows

    for h in range(n_heads):
        qt_ref[h, 0:LANES, :] = _masked_qt(q_ref[0, :, h * LANES:(h + 1) * LANES])

    s_loc = lax.broadcasted_iota(jnp.int32, (t, 2 * t), 0)
    q_both = lax.broadcasted_iota(jnp.int32, (t, 2 * t), 1)
    q_loc = jnp.where(q_both >= t, q_both - t, q_both)
    d_loc = (q_loc - s_loc).astype(F32)
    chunk_mask = (s_loc // CHUNK) <= (q_loc // CHUNK)

    def scores_into(slot, j):
        for h in range(n_heads):
            ss_ref[slot, h] = jnp.dot(kaug_ref[h, j], qt_ref[h], preferred_element_type=F32)

    def stage(j, slot, diag):
        other = 1 - slot
        prod = [jnp.dot(vt_ref[h, jnp.maximum(j - 1, 0)], p_ref[other, h], preferred_element_type=F32)
                for h in range(n_heads)]
        if not diag:
            scores_into(other, j + 1)
        for h in range(n_heads):
            a = ss_ref[slot, h]
            if diag:
                a = a + jnp.where(d_loc < 0.0, (-2.0 * neg_slope(h)) * d_loc, 0.0)
                a = jnp.where(chunk_mask, a, NEG_INF)
            m_old = m_ref[h]
            m_new = jnp.maximum(m_old, jnp.max(a, axis=0, keepdims=True))
            alpha = jnp.exp(m_old - m_new)
            p = jnp.exp(a - m_new)
            l_ref[h] = alpha * l_ref[h] + jnp.sum(p, axis=0, keepdims=True)
            m_ref[h] = m_new
            p_ref[slot, h] = p.astype(BF16)
            acc_ref[h] = alpha * (acc_ref[h] + prod[h])
        if diag:
            for h in range(n_heads):
                acc_ref[h] += jnp.dot(vt_ref[h, j], p_ref[slot, h], preferred_element_type=F32)

    acc_ref[...] = jnp.zeros_like(acc_ref)
    l_ref[...] = jnp.zeros_like(l_ref)
    m_ref[...] = jnp.full_like(m_ref, NEG_INF)
    p_ref[1] = jnp.zeros_like(p_ref[1])
    scores_into(0, 0)

    def body(i, carry):
        stage(2 * i, 0, False)
        stage(2 * i + 1, 1, False)
        return carry

    lax.fori_loop(0, lax.shift_right_logical(qi, 1), body, 0)

    @pl.when(qi % 2 == 1)
    def _():
        stage(qi - 1, 0, False)
        stage(qi, 1, True)

    @pl.when(qi % 2 == 0)
    def _():
        stage(qi, 0, True)

    lambda_init = scal_ref[0]
    lam = (jnp.exp(jnp.sum(lq1_ref[...] * lk1_ref[...], axis=-1, keepdims=True))
           - jnp.exp(jnp.sum(lq2_ref[...] * lk2_ref[...], axis=-1, keepdims=True)) + lambda_init)
    for h in range(n_heads):
        both = acc_ref[h] * (1.0 / l_ref[h])
        o_t = both[:, 0:t] - lam * both[:, t:2 * t]
        o_ref[0, :, h * LANES:(h + 1) * LANES] = (_rms(o_t.T, g_ref[...]) * scal_ref[1]).astype(BF16)


def _da_attention(proj, scal, lq1, lk1, lq2, lk2, g, *, t=128, n_heads=DA_HEADS):
    b, s, _ = proj.shape
    nkb = s // t
    w = n_heads * LANES
    groups = DA_HEADS // n_heads
    q_off = (3 * SB_PAIRS * LANES) // w
    k_off = q_off + groups
    v_off = k_off + groups
    vec = lambda n: pl.BlockSpec((1, n), lambda bg, qi: (0, 0))
    return pl.pallas_call(
        functools.partial(_da_kernel, t=t, n_heads=n_heads,
                          head0_of_group=tuple(gi * n_heads for gi in range(groups))),
        grid=(b * groups, s // t),
        in_specs=[
            pl.BlockSpec(memory_space=pltpu.SMEM),
            vec(HEAD_DIM), vec(HEAD_DIM), vec(HEAD_DIM), vec(HEAD_DIM), vec(LANES),
            pl.BlockSpec((1, t, w), lambda bg, qi: (bg // groups, qi, q_off + bg % groups)),
            pl.BlockSpec((1, s, w), lambda bg, qi: (bg // groups, 0, k_off + bg % groups)),
            pl.BlockSpec((1, s, w), lambda bg, qi: (bg // groups, 0, v_off + bg % groups)),
        ],
        out_specs=pl.BlockSpec((1, t, w), lambda bg, qi: (bg // groups, qi, bg % groups)),
        out_shape=jax.ShapeDtypeStruct((b, s, DA_HEADS * LANES), BF16),
        scratch_shapes=[
            pltpu.VMEM((n_heads, nkb, LANES, t), BF16),
            pltpu.VMEM((n_heads, 2 * LANES, 2 * t), BF16),
            pltpu.VMEM((n_heads, LANES, 2 * t), F32),
            pltpu.VMEM((n_heads, 1, 2 * t), F32),
            pltpu.VMEM((n_heads, 1, 2 * t), F32),
            pltpu.VMEM((n_heads, nkb, t, 2 * LANES), BF16),
            pltpu.VMEM((2, n_heads, t, 2 * t), F32),
            pltpu.VMEM((2, n_heads, t, 2 * t), BF16),
        ],
        compiler_params=pltpu.CompilerParams(
            dimension_semantics=("arbitrary", "arbitrary"), vmem_limit_bytes=VMEM_LIMIT_BYTES),
        name="da_attention",
    )(scal, lq1.reshape(1, -1), lk1.reshape(1, -1), lq2.reshape(1, -1), lk2.reshape(1, -1),
      g.reshape(1, -1), proj, proj, proj)


def _out_ffn_kernel(x_ref, sb_ref, da_ref, mod_ref, wout_ref, apost_ref, fpre_ref, wup_ref, cw_ref, cb_ref,
                    wdown_ref, fpost_ref, o_ref, carry_ref, ubuf_ref, act_ref, *, tm, fc, halo):
    i = pl.program_id(1)
    d_ff = wdown_ref.shape[0]

    @pl.when(i == 0)
    def _():
        carry_ref[...] = jnp.zeros_like(carry_ref)

    m = mod_ref[0]
    n_sb = sb_ref.shape[2]
    mixed = (jnp.dot(sb_ref[0], wout_ref[0:n_sb, :], preferred_element_type=F32)
             + jnp.dot(da_ref[0], wout_ref[n_sb:, :], preferred_element_type=F32))
    x1 = x_ref[0] + m[2:3] * _rms(mixed, apost_ref[...])
    h2 = (_rms(x1, fpre_ref[...]) * (1.0 + m[4:5]) + m[3:4]).astype(BF16)

    for ci in range(d_ff // fc):
        halves = []
        for part in range(2):
            col = part * d_ff + ci * fc
            u = jnp.dot(h2, wup_ref[:, col:col + fc], preferred_element_type=F32)
            ubuf_ref[0:halo, :] = carry_ref[:, col:col + fc]
            ubuf_ref[halo:halo + tm, :] = u
            carry_ref[:, col:col + fc] = u[tm - halo:tm]
            cw = cw_ref[:, col:col + fc]
            y = cb_ref[:, col:col + fc] + cw[CONV_WIDTH - 1:CONV_WIDTH] * u
            for tap in range(CONV_WIDTH - 1):
                back = CONV_WIDTH - 1 - tap
                y = y + cw[tap:tap + 1] * ubuf_ref[pl.ds(halo - back, tm), :]
            halves.append(y)
        gate, val = halves
        act_ref[:, ci * fc:(ci + 1) * fc] = (gate * jax.nn.sigmoid(gate) * val).astype(BF16)
    f = jnp.dot(act_ref[...], wdown_ref[...], preferred_element_type=F32)
    o_ref[0] = x1 + m[5:6] * _rms(f, fpost_ref[...])


def _out_ffn(x, sb, da, mod_l, w_out, apost, fpre, w_up, conv_w, conv_b, w_down, fpost, *, tm=512, fc=256):
    b, s, d = x.shape
    d_ff = w_down.shape[0]
    halo = 8
    const = lambda shape: pl.BlockSpec(shape, lambda bi, i: (0,) * len(shape), pipeline_mode=pl.Buffered(1))
    row = lambda n: pl.BlockSpec((1, tm, n), lambda bi, i: (bi, i, 0))
    return pl.pallas_call(
        functools.partial(_out_ffn_kernel, tm=tm, fc=fc, halo=halo),
        grid=(b, s // tm),
        in_specs=[
            row(d), row(sb.shape[2]), row(da.shape[2]),
            pl.BlockSpec((1, N_MOD, d), lambda bi, i: (bi, 0, 0)),
            const(w_out.shape), const((1, d)), const((1, d)),
            const(w_up.shape), const(conv_w.shape), const((1, 2 * d_ff)),
            const(w_down.shape), const((1, d)),
        ],
        out_specs=row(d),
        out_shape=jax.ShapeDtypeStruct((b, s, d), F32),
        scratch_shapes=[
            pltpu.VMEM((halo, 2 * d_ff), F32),
            pltpu.VMEM((halo + tm, fc), F32),
            pltpu.VMEM((tm, d_ff), BF16),
        ],
        compiler_params=pltpu.CompilerParams(
            dimension_semantics=("arbitrary", "arbitrary"), vmem_limit_bytes=VMEM_LIMIT_BYTES),
        name="out_ffn",
    )(x, sb, da, mod_l, w_out, apost.reshape(1, d), fpre.reshape(1, d), w_up, conv_w,
      conv_b.reshape(1, 2 * d_ff), w_down, fpost.reshape(1, d))


def kernel(x, c, ada_w, ada_b, attn_pre_g, attn_post_g, w_in, w_out, lambda_q1, lambda_k1, lambda_q2, lambda_k2,
           da_subln_g, ffn_pre_g, ffn_post_g, w_up, conv_w, conv_b, w_down):
    depth = ada_w.shape[0]
    b, s, d = x.shape
    mod = _ada_mod(c, ada_w, ada_b).reshape(depth, b, N_MOD, d)
    w_in, w_out, w_up, w_down = (w.astype(BF16) for w in (w_in, w_out, w_up, w_down))
    for l in range(depth):
        proj = _in_proj(x, mod[l], attn_pre_g[l], w_in[l])
        sb = _sb_attention(proj)
        lambda_init = 0.8 - 0.6 * math.exp(-0.3 * l)
        scal = jnp.array([lambda_init, 1.0 - lambda_init], F32)
        da = _da_attention(proj, scal, lambda_q1[l], lambda_k1[l], lambda_q2[l], lambda_k2[l], da_subln_g[l])
        x = _out_ffn(x, sb, da, mod[l], w_out[l], attn_post_g[l], ffn_pre_g[l], w_up[l], conv_w[l], conv_b[l],
                     w_down[l], ffn_post_g[l])
    return x
```

```python
import functools
import math

import jax
import jax.numpy as jnp
from jax import lax
from jax.experimental import pallas as pl
from jax.experimental.pallas import tpu as pltpu

F32 = jnp.float32
BF16 = jnp.bfloat16

CHUNK = 64
SB_HEADS = 8
HEAD_DIM = 64
DA_HEADS = 4
N_MOD = 6
CONV_WIDTH = 3
EPS = 1e-6
NEG_INF = -1e30

LANES = 128
SB_PAIRS = SB_HEADS // 2
VMEM_LIMIT_BYTES = 56 * 1024 * 1024

SB_SKIP_THRESHOLD = 110.0
SB_SKEW = 2


def _rms(x, g):
    return x * lax.rsqrt(jnp.mean(x * x, axis=-1, keepdims=True) + EPS) * g


def _ada_kernel(c_ref, w_ref, b_ref, o_ref):
    c = c_ref[...]
    c_act = (c * jax.nn.sigmoid(c)).astype(BF16)
    o_ref[0] = jnp.dot(c_act, w_ref[0].astype(BF16), preferred_element_type=F32) + b_ref[0]


def _ada_mod(c, ada_w, ada_b, *, tn=1536):
    depth, d, n = ada_w.shape
    b = c.shape[0]
    return pl.pallas_call(
        _ada_kernel,
        grid=(depth, n // tn),
        in_specs=[
            pl.BlockSpec((b, d), lambda l, j: (0, 0)),
            pl.BlockSpec((1, d, tn), lambda l, j: (l, 0, j)),
            pl.BlockSpec((1, 1, tn), lambda l, j: (l, 0, j)),
        ],
        out_specs=pl.BlockSpec((1, b, tn), lambda l, j: (l, 0, j)),
        out_shape=jax.ShapeDtypeStruct((depth, b, n), F32),
        compiler_params=pltpu.CompilerParams(
            dimension_semantics=("arbitrary", "arbitrary"), vmem_limit_bytes=VMEM_LIMIT_BYTES),
        name="ada_mod",
    )(c, ada_w, ada_b.reshape(depth, 1, n))


def _in_proj_kernel(x_ref, mod_ref, g_ref, w_ref, o_ref, *, n_chunk):
    m = mod_ref[0]
    h = (_rms(x_ref[0], g_ref[...]) * (1.0 + m[1:2]) + m[0:1]).astype(BF16)
    n = w_ref.shape[1]
    for n0 in range(0, n, n_chunk):
        o_ref[0, :, n0:n0 + n_chunk] = jnp.dot(
            h, w_ref[:, n0:n0 + n_chunk], preferred_element_type=F32).astype(BF16)


def _in_proj(x, mod_l, g, w, *, tm=512, n_chunk=512):
    b, s, d = x.shape
    n = w.shape[1]
    return pl.pallas_call(
        functools.partial(_in_proj_kernel, n_chunk=n_chunk),
        grid=(b, s // tm),
        in_specs=[
            pl.BlockSpec((1, tm, d), lambda bi, i: (bi, i, 0)),
            pl.BlockSpec((1, N_MOD, d), lambda bi, i: (bi, 0, 0)),
            pl.BlockSpec((1, d), lambda bi, i: (0, 0)),
            pl.BlockSpec((d, n), lambda bi, i: (0, 0)),
        ],
        out_specs=pl.BlockSpec((1, tm, n), lambda bi, i: (bi, i, 0)),
        out_shape=jax.ShapeDtypeStruct((b, s, n), BF16),
        compiler_params=pltpu.CompilerParams(
            dimension_semantics=("arbitrary", "arbitrary"), vmem_limit_bytes=VMEM_LIMIT_BYTES),
        name="in_proj",
    )(x, mod_l, g.reshape(1, d), w)


def _split_halves_t(blk):
    bt = blk.astype(F32).T
    row = lax.broadcasted_iota(jnp.int32, bt.shape, 0)
    return jnp.where(row < HEAD_DIM, bt, 0.0), jnp.where(row >= HEAD_DIM, bt, 0.0)


def _masked_qt(q_blk):
    lo, hi = _split_halves_t(q_blk * jnp.asarray(HEAD_DIM ** -0.5, BF16))
    return jnp.concatenate([lo, hi], axis=1).astype(BF16)


def _sb_kernel(q_ref, k_ref, v_ref, o_ref, vt_ref, qt_ref, acc_ref, c_ref, zz_ref, ws_ref, *, t, n_pairs):
    qi = pl.program_id(1)
    nkb = vt_ref.shape[1]

    @pl.when(qi == 0)
    def _():
        def body(j, carry):
            start = pl.multiple_of(j * t, t)
            for p in range(n_pairs):
                lo, hi = _split_halves_t(v_ref[0, pl.ds(start, t), p * LANES:(p + 1) * LANES])
                vt_ref[p, j] = jnp.concatenate([lo, hi], axis=1).astype(BF16)
            return carry

        lax.fori_loop(0, nkb, body, 0)

    for p in range(n_pairs):
        qt_ref[p] = _masked_qt(q_ref[0, :, p * LANES:(p + 1) * LANES])
    acc_ref[...] = jnp.zeros_like(acc_ref)
    c_ref[...] = jnp.zeros_like(c_ref)

    tq = 2 * t
    s_loc = lax.broadcasted_iota(jnp.int32, (t, tq), 0)
    q_loc = lax.broadcasted_iota(jnp.int32, (t, tq), 1)
    tri = (lax.broadcasted_iota(jnp.int32, (t, t), 1)
           >= lax.broadcasted_iota(jnp.int32, (t, t), 0)).astype(BF16)
    n_heads = 2 * n_pairs

    def scores_into(slot, j):
        start = pl.multiple_of(j * t, t)
        for p in range(n_pairs):
            zz_ref[slot, p] = jnp.dot(k_ref[0, pl.ds(start, t), p * LANES:(p + 1) * LANES], qt_ref[p],
                                      preferred_element_type=F32)

    def stage(j, slot, key_off=None):
        other = 1 - slot
        c_in = [c_ref[p] for p in range(n_pairs)]
        smallest = jnp.min(functools.reduce(jnp.minimum, c_in))
        causal = None if key_off is None else (s_loc + key_off) < q_loc
        cols_of = lambda i: slice((i % 2) * tq, (i % 2 + 1) * tq)
        suffix, prod = [None] * n_heads, None
        for step in range(n_heads + SB_SKEW):
            if step < n_heads:
                z = zz_ref[slot, step // 2, :, cols_of(step)]
                sp = jnp.where(z > 80.0, z, jnp.log(1.0 + jnp.exp(z)))
                if causal is not None:
                    sp = jnp.where(causal, sp, 0.0)
                suffix[step] = jnp.dot(tri, sp.astype(BF16), preferred_element_type=F32)
            if step == 0:
                prod = [jnp.dot(vt_ref[p, jnp.minimum(j + 1, nkb - 1)], ws_ref[other, p],
                                preferred_element_type=F32) for p in range(n_pairs)]
                scores_into(other, jnp.maximum(j - 1, 0))
            i = step - SB_SKEW
            if i >= 0:
                cs = suffix[i] + c_in[i // 2][:, cols_of(i)]
                w = jnp.exp(zz_ref[slot, i // 2, :, cols_of(i)] - cs)
                if causal is not None:
                    w = jnp.where(causal, w, 0.0)
                ws_ref[slot, i // 2, (i % 2) * t:(i % 2 + 1) * t, :] = w.astype(BF16)
                c_ref[i // 2, :, cols_of(i)] = cs[0:1, :]
        for p in range(n_pairs):
            acc_ref[p] += prod[p]
        return smallest

    ws_ref[...] = jnp.zeros_like(ws_ref)

    scores_into(1, 2 * qi + 1)
    stage(2 * qi + 1, 1, key_off=t)
    smallest = stage(2 * qi, 0, key_off=0)

    def cond(state):
        j, smallest = state
        return jnp.logical_and(j >= 1, smallest < SB_SKIP_THRESHOLD)

    def body(state):
        j = state[0]
        stage(j, 1)
        return j - 2, stage(j - 1, 0)

    j_end, _ = lax.while_loop(cond, body, (2 * qi - 1, smallest))
    last = j_end + 1
    for p in range(n_pairs):
        acc_ref[p] += jnp.dot(vt_ref[p, last], ws_ref[0, p], preferred_element_type=F32)

    for p in range(n_pairs):
        o_ref[0, :, p * LANES:(p + 1) * LANES] = acc_ref[p].T.astype(BF16)


def _sb_attention(proj, *, t=128, n_pairs=SB_PAIRS):
    b, s, _ = proj.shape
    nkb = s // t
    tq = 2 * t
    w = n_pairs * LANES
    groups = SB_PAIRS // n_pairs
    return pl.pallas_call(
        functools.partial(_sb_kernel, t=t, n_pairs=n_pairs),
        grid=(b * groups, s // tq),
        in_specs=[
            pl.BlockSpec((1, tq, w), lambda bg, qi: (bg // groups, qi, bg % groups)),
            pl.BlockSpec((1, s, w), lambda bg, qi: (bg // groups, 0, groups + bg % groups)),
            pl.BlockSpec((1, s, w), lambda bg, qi: (bg // groups, 0, 2 * groups + bg % groups)),
        ],
        out_specs=pl.BlockSpec((1, tq, w), lambda bg, qi: (bg // groups, qi, bg % groups)),
        out_shape=jax.ShapeDtypeStruct((b, s, SB_PAIRS * LANES), BF16),
        scratch_shapes=[
            pltpu.VMEM((n_pairs, nkb, LANES, 2 * t), BF16),
            pltpu.VMEM((n_pairs, LANES, 2 * tq), BF16),
            pltpu.VMEM((n_pairs, LANES, tq), F32),
            pltpu.VMEM((n_pairs, 1, 2 * tq), F32),
            pltpu.VMEM((2, n_pairs, t, 2 * tq), F32),
            pltpu.VMEM((2, n_pairs, 2 * t, tq), BF16),
        ],
        compiler_params=pltpu.CompilerParams(
            dimension_semantics=("arbitrary", "arbitrary"), vmem_limit_bytes=VMEM_LIMIT_BYTES),
        name="sb_attention",
    )(proj, proj, proj)


def _da_kernel(scal_ref, lq1_ref, lk1_ref, lq2_ref, lk2_ref, g_ref, q_ref, k_ref, v_ref, o_ref,
               vt_ref, qt_ref, acc_ref, m_ref, l_ref, kaug_ref, ss_ref, p_ref, *, t, n_heads, head0_of_group):
    qi = pl.program_id(1)
    nkb = vt_ref.shape[1]
    tq = 2 * t
    group = pl.program_id(0) % len(head0_of_group)

    def neg_slope(h):
        vals = [-(2.0 ** (-8.0 * (h0 + h + 1) / DA_HEADS)) for h0 in head0_of_group]
        out = jnp.float32(vals[-1])
        for gi in range(len(vals) - 2, -1, -1):
            out = jnp.where(group == gi, jnp.float32(vals[gi]), out)
        return out

    @pl.when(qi == 0)
    def _():
        lane = lax.broadcasted_iota(jnp.int32, (t, LANES), 1)
        s_row = lax.broadcasted_iota(jnp.int32, (t, LANES), 0).astype(F32)
        ones_rows = (lax.broadcasted_iota(jnp.int32, (LANES, 2 * tq), 0) < 2).astype(BF16)

        def body(j, carry):
            start = pl.multiple_of(j * t, t)
            base = jnp.asarray(j * t, F32)
            for h in range(n_heads):
                cols = slice(h * LANES, (h + 1) * LANES)
                vt_ref[h, j] = v_ref[0, pl.ds(start, t), cols].astype(F32).T.astype(BF16)
                slope = -neg_slope(h)
                extra = jnp.where(lane == 0, slope * base, jnp.where(lane == 1, slope * s_row, 0.0))
                kaug_ref[h, j, :, 0:LANES] = k_ref[0, pl.ds(start, t), cols]
                kaug_ref[h, j, :, LANES:2 * LANES] = extra.astype(BF16)
            return carry

        lax.fori_loop(0, nkb, body, 0)
        for h in range(n_heads):
            qt_ref[h, LANES:2 * LANES, :] = ones_rows

    for h in range(n_heads):
        qt_ref[h, 0:LANES, :] = _masked_qt(q_ref[0, :, h * LANES:(h + 1) * LANES])

    s_loc = lax.broadcasted_iota(jnp.int32, (t, tq), 0)
    q_loc = lax.broadcasted_iota(jnp.int32, (t, tq), 1)

    def scores_into(slot, j):
        for h in range(n_heads):
            ss_ref[slot, h] = jnp.dot(kaug_ref[h, j], qt_ref[h], preferred_element_type=F32)

    def stage(j, slot, key_off=None, last=False):
        other = 1 - slot
        prod = [jnp.dot(vt_ref[h, jnp.maximum(j - 1, 0)], p_ref[other, h], preferred_element_type=F32)
                for h in range(n_heads)]
        if not last:
            scores_into(other, j + 1)
        if key_off is not None:
            s_pos = s_loc + key_off
            d_loc = (q_loc - s_pos).astype(F32)
            chunk_mask = (s_pos // CHUNK) <= (q_loc // CHUNK)
        for h in range(n_heads):
            for half in range(2):
                cols = slice(half * tq, (half + 1) * tq)
                a = ss_ref[slot, h, :, cols]
                if key_off is not None:
                    a = a + jnp.where(d_loc < 0.0, (-2.0 * neg_slope(h)) * d_loc, 0.0)
                    a = jnp.where(chunk_mask, a, NEG_INF)
                m_old = m_ref[h, :, cols]
                m_new = jnp.maximum(m_old, jnp.max(a, axis=0, keepdims=True))
                alpha = jnp.exp(m_old - m_new)
                p = jnp.exp(a - m_new)
                l_ref[h, :, cols] = alpha * l_ref[h, :, cols] + jnp.sum(p, axis=0, keepdims=True)
                m_ref[h, :, cols] = m_new
                p_ref[slot, h, :, cols] = p.astype(BF16)
                acc_ref[h, :, cols] = alpha * (acc_ref[h, :, cols] + prod[h][:, cols])
        if last:
            for h in range(n_heads):
                acc_ref[h] += jnp.dot(vt_ref[h, j], p_ref[slot, h], preferred_element_type=F32)

    acc_ref[...] = jnp.zeros_like(acc_ref)
    l_ref[...] = jnp.zeros_like(l_ref)
    m_ref[...] = jnp.full_like(m_ref, NEG_INF)
    p_ref[1] = jnp.zeros_like(p_ref[1])
    scores_into(0, 0)

    def body(i, carry):
        stage(2 * i, 0)
        stage(2 * i + 1, 1)
        return carry

    lax.fori_loop(0, qi, body, 0)
    stage(2 * qi, 0, key_off=0)
    stage(2 * qi + 1, 1, key_off=t, last=True)

    lambda_init = scal_ref[0]
    lam = (jnp.exp(jnp.sum(lq1_ref[...] * lk1_ref[...], axis=-1, keepdims=True))
           - jnp.exp(jnp.sum(lq2_ref[...] * lk2_ref[...], axis=-1, keepdims=True)) + lambda_init)
    for h in range(n_heads):
        both = acc_ref[h] * (1.0 / l_ref[h])
        o_t = both[:, 0:tq] - lam * both[:, tq:2 * tq]
        o_ref[0, :, h * LANES:(h + 1) * LANES] = (_rms(o_t.T, g_ref[...]) * scal_ref[1]).astype(BF16)


def _da_attention(proj, scal, lq1, lk1, lq2, lk2, g, *, t=128, n_heads=DA_HEADS):
    b, s, _ = proj.shape
    nkb = s // t
    tq = 2 * t
    w = n_heads * LANES
    groups = DA_HEADS // n_heads
    q_off = (3 * SB_PAIRS * LANES) // w
    k_off = q_off + groups
    v_off = k_off + groups
    vec = lambda n: pl.BlockSpec((1, n), lambda bg, qi: (0, 0))
    return pl.pallas_call(
        functools.partial(_da_kernel, t=t, n_heads=n_heads,
                          head0_of_group=tuple(gi * n_heads for gi in range(groups))),
        grid=(b * groups, s // tq),
        in_specs=[
            pl.BlockSpec(memory_space=pltpu.SMEM),
            vec(HEAD_DIM), vec(HEAD_DIM), vec(HEAD_DIM), vec(HEAD_DIM), vec(LANES),
            pl.BlockSpec((1, tq, w), lambda bg, qi: (bg // groups, qi, q_off + bg % groups)),
            pl.BlockSpec((1, s, w), lambda bg, qi: (bg // groups, 0, k_off + bg % groups)),
            pl.BlockSpec((1, s, w), lambda bg, qi: (bg // groups, 0, v_off + bg % groups)),
        ],
        out_specs=pl.BlockSpec((1, tq, w), lambda bg, qi: (bg // groups, qi, bg % groups)),
        out_shape=jax.ShapeDtypeStruct((b, s, DA_HEADS * LANES), BF16),
        scratch_shapes=[
            pltpu.VMEM((n_heads, nkb, LANES, t), BF16),
            pltpu.VMEM((n_heads, 2 * LANES, 2 * tq), BF16),
            pltpu.VMEM((n_heads, LANES, 2 * tq), F32),
            pltpu.VMEM((n_heads, 1, 2 * tq), F32),
            pltpu.VMEM((n_heads, 1, 2 * tq), F32),
            pltpu.VMEM((n_heads, nkb, t, 2 * LANES), BF16),
            pltpu.VMEM((2, n_heads, t, 2 * tq), F32),
            pltpu.VMEM((2, n_heads, t, 2 * tq), BF16),
        ],
        compiler_params=pltpu.CompilerParams(
            dimension_semantics=("arbitrary", "arbitrary"), vmem_limit_bytes=VMEM_LIMIT_BYTES),
        name="da_attention",
    )(scal, lq1.reshape(1, -1), lk1.reshape(1, -1), lq2.reshape(1, -1), lk2.reshape(1, -1),
      g.reshape(1, -1), proj, proj, proj)


def _out_ffn_kernel(x_ref, sb_ref, da_ref, mod_ref, wout_ref, apost_ref, fpre_ref, wup_ref, cw_ref, cb_ref,
                    wdown_ref, fpost_ref, o_ref, carry_ref, ubuf_ref, act_ref, *, tm, fc, halo):
    i = pl.program_id(1)
    d_ff = wdown_ref.shape[0]

    @pl.when(i == 0)
    def _():
        carry_ref[...] = jnp.zeros_like(carry_ref)

    m = mod_ref[0]
    n_sb = sb_ref.shape[2]
    mixed = (jnp.dot(sb_ref[0], wout_ref[0:n_sb, :], preferred_element_type=F32)
             + jnp.dot(da_ref[0], wout_ref[n_sb:, :], preferred_element_type=F32))
    x1 = x_ref[0] + m[2:3] * _rms(mixed, apost_ref[...])
    h2 = (_rms(x1, fpre_ref[...]) * (1.0 + m[4:5]) + m[3:4]).astype(BF16)

    for ci in range(d_ff // fc):
        halves = []
        for part in range(2):
            col = part * d_ff + ci * fc
            u = jnp.dot(h2, wup_ref[:, col:col + fc], preferred_element_type=F32)
            ubuf_ref[0:halo, :] = carry_ref[:, col:col + fc]
            ubuf_ref[halo:halo + tm, :] = u
            carry_ref[:, col:col + fc] = u[tm - halo:tm]
            cw = cw_ref[:, col:col + fc]
            y = cb_ref[:, col:col + fc] + cw[CONV_WIDTH - 1:CONV_WIDTH] * u
            for tap in range(CONV_WIDTH - 1):
                back = CONV_WIDTH - 1 - tap
                y = y + cw[tap:tap + 1] * ubuf_ref[pl.ds(halo - back, tm), :]
            halves.append(y)
        gate, val = halves
        act_ref[:, ci * fc:(ci + 1) * fc] = (gate * jax.nn.sigmoid(gate) * val).astype(BF16)
    f = jnp.dot(act_ref[...], wdown_ref[...], preferred_element_type=F32)
    o_ref[0] = x1 + m[5:6] * _rms(f, fpost_ref[...])


def _out_ffn(x, sb, da, mod_l, w_out, apost, fpre, w_up, conv_w, conv_b, w_down, fpost, *, tm=512, fc=256):
    b, s, d = x.shape
    d_ff = w_down.shape[0]
    halo = 8
    const = lambda shape: pl.BlockSpec(shape, lambda bi, i: (0,) * len(shape), pipeline_mode=pl.Buffered(1))
    row = lambda n: pl.BlockSpec((1, tm, n), lambda bi, i: (bi, i, 0))
    return pl.pallas_call(
        functools.partial(_out_ffn_kernel, tm=tm, fc=fc, halo=halo),
        grid=(b, s // tm),
        in_specs=[
            row(d), row(sb.shape[2]), row(da.shape[2]),
            pl.BlockSpec((1, N_MOD, d), lambda bi, i: (bi, 0, 0)),
            const(w_out.shape), const((1, d)), const((1, d)),
            const(w_up.shape), const(conv_w.shape), const((1, 2 * d_ff)),
            const(w_down.shape), const((1, d)),
        ],
        out_specs=row(d),
        out_shape=jax.ShapeDtypeStruct((b, s, d), F32),
        scratch_shapes=[
            pltpu.VMEM((halo, 2 * d_ff), F32),
            pltpu.VMEM((halo + tm, fc), F32),
            pltpu.VMEM((tm, d_ff), BF16),
        ],
        compiler_params=pltpu.CompilerParams(
            dimension_semantics=("arbitrary", "arbitrary"), vmem_limit_bytes=VMEM_LIMIT_BYTES),
        name="out_ffn",
    )(x, sb, da, mod_l, w_out, apost.reshape(1, d), fpre.reshape(1, d), w_up, conv_w,
      conv_b.reshape(1, 2 * d_ff), w_down, fpost.reshape(1, d))


def kernel(x, c, ada_w, ada_b, attn_pre_g, attn_post_g, w_in, w_out, lambda_q1, lambda_k1, lambda_q2, lambda_k2,
           da_subln_g, ffn_pre_g, ffn_post_g, w_up, conv_w, conv_b, w_down):
    depth = ada_w.shape[0]
    b, s, d = x.shape
    mod = _ada_mod(c, ada_w, ada_b).reshape(depth, b, N_MOD, d)
    w_in, w_out, w_up, w_down = (w.astype(BF16) for w in (w_in, w_out, w_up, w_down))
    for l in range(depth):
        proj = _in_proj(x, mod[l], attn_pre_g[l], w_in[l])
        sb = _sb_attention(proj)
        lambda_init = 0.8 - 0.6 * math.exp(-0.3 * l)
        scal = jnp.array([lambda_init, 1.0 - lambda_init], F32)
        da = _da_attention(proj, scal, lambda_q1[l], lambda_k1[l], lambda_q2[l], lambda_k2[l], da_subln_g[l])
        x = _out_ffn(x, sb, da, mod[l], w_out[l], attn_post_g[l], ffn_pre_g[l], w_up[l], conv_w[l], conv_b[l],
                     w_down[l], ffn_post_g[l])
    return x
```

```python
import functools
import math

import jax
import jax.numpy as jnp
from jax import lax
from jax.experimental import pallas as pl
from jax.experimental.pallas import tpu as pltpu

F32 = jnp.float32
BF16 = jnp.bfloat16

CHUNK = 64
SB_HEADS = 8
HEAD_DIM = 64
DA_HEADS = 4
N_MOD = 6
CONV_WIDTH = 3
EPS = 1e-6
NEG_INF = -1e30

LANES = 128
SB_PAIRS = SB_HEADS // 2
VMEM_LIMIT_BYTES = 56 * 1024 * 1024

SB_SKIP_THRESHOLD = 110.0
SB_SKEW = 2


def _rms(x, g):
    return x * lax.rsqrt(jnp.mean(x * x, axis=-1, keepdims=True) + EPS) * g


def _ada_kernel(c_ref, w_ref, b_ref, o_ref):
    c = c_ref[...]
    c_act = (c * jax.nn.sigmoid(c)).astype(BF16)
    o_ref[0] = jnp.dot(c_act, w_ref[0].astype(BF16), preferred_element_type=F32) + b_ref[0]


def _ada_mod(c, ada_w, ada_b, *, tn=1536):
    depth, d, n = ada_w.shape
    b = c.shape[0]
    return pl.pallas_call(
        _ada_kernel,
        grid=(depth, n // tn),
        in_specs=[
            pl.BlockSpec((b, d), lambda l, j: (0, 0)),
            pl.BlockSpec((1, d, tn), lambda l, j: (l, 0, j)),
            pl.BlockSpec((1, 1, tn), lambda l, j: (l, 0, j)),
        ],
        out_specs=pl.BlockSpec((1, b, tn), lambda l, j: (l, 0, j)),
        out_shape=jax.ShapeDtypeStruct((depth, b, n), F32),
        compiler_params=pltpu.CompilerParams(
            dimension_semantics=("arbitrary", "arbitrary"), vmem_limit_bytes=VMEM_LIMIT_BYTES),
        name="ada_mod",
    )(c, ada_w, ada_b.reshape(depth, 1, n))


def _in_proj_kernel(x_ref, mod_ref, g_ref, w_ref, o_ref, *, n_chunk):
    m = mod_ref[0]
    h = (_rms(x_ref[0], g_ref[...]) * (1.0 + m[1:2]) + m[0:1]).astype(BF16)
    n = w_ref.shape[1]
    for n0 in range(0, n, n_chunk):
        o_ref[0, :, n0:n0 + n_chunk] = jnp.dot(
            h, w_ref[:, n0:n0 + n_chunk], preferred_element_type=F32).astype(BF16)


def _in_proj(x, mod_l, g, w, *, tm=512, n_chunk=512):
    b, s, d = x.shape
    n = w.shape[1]
    return pl.pallas_call(
        functools.partial(_in_proj_kernel, n_chunk=n_chunk),
        grid=(b, s // tm),
        in_specs=[
            pl.BlockSpec((1, tm, d), lambda bi, i: (bi, i, 0)),
            pl.BlockSpec((1, N_MOD, d), lambda bi, i: (bi, 0, 0)),
            pl.BlockSpec((1, d), lambda bi, i: (0, 0)),
            pl.BlockSpec((d, n), lambda bi, i: (0, 0)),
        ],
        out_specs=pl.BlockSpec((1, tm, n), lambda bi, i: (bi, i, 0)),
        out_shape=jax.ShapeDtypeStruct((b, s, n), BF16),
        compiler_params=pltpu.CompilerParams(
            dimension_semantics=("arbitrary", "arbitrary"), vmem_limit_bytes=VMEM_LIMIT_BYTES),
        name="in_proj",
    )(x, mod_l, g.reshape(1, d), w)


def _split_halves_t(blk):
    bt = blk.astype(F32).T
    row = lax.broadcasted_iota(jnp.int32, bt.shape, 0)
    return jnp.where(row < HEAD_DIM, bt, 0.0), jnp.where(row >= HEAD_DIM, bt, 0.0)


def _masked_qt(q_blk):
    lo, hi = _split_halves_t(q_blk * jnp.asarray(HEAD_DIM ** -0.5, BF16))
    return jnp.concatenate([lo, hi], axis=1).astype(BF16)


def _sb_kernel(q_ref, k_ref, v_ref, o_ref, vt_ref, qt_ref, acc_ref, c_ref, zz_ref, ws_ref, *, t, n_pairs):
    qi = pl.program_id(1)
    nkb = vt_ref.shape[1]

    @pl.when(qi == 0)
    def _():
        def body(j, carry):
            start = pl.multiple_of(j * t, t)
            for p in range(n_pairs):
                lo, hi = _split_halves_t(v_ref[0, pl.ds(start, t), p * LANES:(p + 1) * LANES])
                vt_ref[p, j] = jnp.concatenate([lo, hi], axis=1).astype(BF16)
            return carry

        lax.fori_loop(0, nkb, body, 0)

    for p in range(n_pairs):
        qt_ref[p] = _masked_qt(q_ref[0, :, p * LANES:(p + 1) * LANES])
    acc_ref[...] = jnp.zeros_like(acc_ref)
    c_ref[...] = jnp.zeros_like(c_ref)

    tq = 2 * t
    s_loc = lax.broadcasted_iota(jnp.int32, (t, tq), 0)
    q_loc = lax.broadcasted_iota(jnp.int32, (t, tq), 1)
    tri = (lax.broadcasted_iota(jnp.int32, (t, t), 1)
           >= lax.broadcasted_iota(jnp.int32, (t, t), 0)).astype(BF16)
    n_heads = 2 * n_pairs

    def scores_into(slot, j):
        start = pl.multiple_of(j * t, t)
        for p in range(n_pairs):
            zz_ref[slot, p] = jnp.dot(k_ref[0, pl.ds(start, t), p * LANES:(p + 1) * LANES], qt_ref[p],
                                      preferred_element_type=F32)

    def stage(j, slot, key_off=None):
        other = 1 - slot
        c_in = [c_ref[p] for p in range(n_pairs)]
        smallest = jnp.min(functools.reduce(jnp.minimum, c_in))
        causal = None if key_off is None else (s_loc + key_off) < q_loc
        cols_of = lambda i: slice((i % 2) * tq, (i % 2 + 1) * tq)
        suffix, prod = [None] * n_heads, None
        for step in range(n_heads + SB_SKEW):
            if step < n_heads:
                z = zz_ref[slot, step // 2, :, cols_of(step)]
                sp = jnp.where(z > 80.0, z, jnp.log(1.0 + jnp.exp(z)))
                if causal is not None:
                    sp = jnp.where(causal, sp, 0.0)
                suffix[step] = jnp.dot(tri, sp.astype(BF16), preferred_element_type=F32)
            if step == 0:
                prod = [jnp.dot(vt_ref[p, jnp.minimum(j + 1, nkb - 1)], ws_ref[other, p],
                                preferred_element_type=F32) for p in range(n_pairs)]
                scores_into(other, jnp.maximum(j - 1, 0))
            i = step - SB_SKEW
            if i >= 0:
                cs = suffix[i] + c_in[i // 2][:, cols_of(i)]
                w = jnp.exp(zz_ref[slot, i // 2, :, cols_of(i)] - cs)
                if causal is not None:
                    w = jnp.where(causal, w, 0.0)
                ws_ref[slot, i // 2, (i % 2) * t:(i % 2 + 1) * t, :] = w.astype(BF16)
                c_ref[i // 2, :, cols_of(i)] = cs[0:1, :]
        for p in range(n_pairs):
            acc_ref[p] += prod[p]
        return smallest

    ws_ref[...] = jnp.zeros_like(ws_ref)

    scores_into(1, 2 * qi + 1)
    stage(2 * qi + 1, 1, key_off=t)
    smallest = stage(2 * qi, 0, key_off=0)

    def cond(state):
        j, smallest = state
        return jnp.logical_and(j >= 1, smallest < SB_SKIP_THRESHOLD)

    def body(state):
        j = state[0]
        stage(j, 1)
        return j - 2, stage(j - 1, 0)

    j_end, _ = lax.while_loop(cond, body, (2 * qi - 1, smallest))
    last = j_end + 1
    for p in range(n_pairs):
        acc_ref[p] += jnp.dot(vt_ref[p, last], ws_ref[0, p], preferred_element_type=F32)

    for p in range(n_pairs):
        o_ref[0, :, p * LANES:(p + 1) * LANES] = acc_ref[p].T.astype(BF16)


def _sb_attention(proj, *, t=128, n_pairs=SB_PAIRS):
    b, s, _ = proj.shape
    nkb = s // t
    tq = 2 * t
    w = n_pairs * LANES
    groups = SB_PAIRS // n_pairs
    return pl.pallas_call(
        functools.partial(_sb_kernel, t=t, n_pairs=n_pairs),
        grid=(b * groups, s // tq),
        in_specs=[
            pl.BlockSpec((1, tq, w), lambda bg, qi: (bg // groups, qi, bg % groups)),
            pl.BlockSpec((1, s, w), lambda bg, qi: (bg // groups, 0, groups + bg % groups)),
            pl.BlockSpec((1, s, w), lambda bg, qi: (bg // groups, 0, 2 * groups + bg % groups)),
        ],
        out_specs=pl.BlockSpec((1, tq, w), lambda bg, qi: (bg // groups, qi, bg % groups)),
        out_shape=jax.ShapeDtypeStruct((b, s, SB_PAIRS * LANES), BF16),
        scratch_shapes=[
            pltpu.VMEM((n_pairs, nkb, LANES, 2 * t), BF16),
            pltpu.VMEM((n_pairs, LANES, 2 * tq), BF16),
            pltpu.VMEM((n_pairs, LANES, tq), F32),
            pltpu.VMEM((n_pairs, 1, 2 * tq), F32),
            pltpu.VMEM((2, n_pairs, t, 2 * tq), F32),
            pltpu.VMEM((2, n_pairs, 2 * t, tq), BF16),
        ],
        compiler_params=pltpu.CompilerParams(
            dimension_semantics=("arbitrary", "arbitrary"), vmem_limit_bytes=VMEM_LIMIT_BYTES),
        name="sb_attention",
    )(proj, proj, proj)


def _da_kernel(scal_ref, lq1_ref, lk1_ref, lq2_ref, lk2_ref, g_ref, q_ref, k_ref, v_ref, o_ref,
               vt_ref, qt_ref, acc_ref, m_ref, l_ref, kaug_ref, ss_ref, p_ref, *, t, n_heads, head0_of_group):
    qi = pl.program_id(1)
    nkb = kaug_ref.shape[1]
    tq = 2 * t
    group = pl.program_id(0) % len(head0_of_group)

    def neg_slope(h):
        vals = [-(2.0 ** (-8.0 * (h0 + h + 1) / DA_HEADS)) for h0 in head0_of_group]
        out = jnp.float32(vals[-1])
        for gi in range(len(vals) - 2, -1, -1):
            out = jnp.where(group == gi, jnp.float32(vals[gi]), out)
        return out

    @pl.when(qi == 0)
    def _():
        lane = lax.broadcasted_iota(jnp.int32, (t, LANES), 1)
        s_row = lax.broadcasted_iota(jnp.int32, (t, LANES), 0).astype(F32)
        ones_rows = (lax.broadcasted_iota(jnp.int32, (LANES, 2 * tq), 0) < 2).astype(BF16)

        def body(j, carry):
            start = pl.multiple_of(j * t, t)
            base = jnp.asarray(j * t, F32)
            for h in range(n_heads):
                cols = slice(h * LANES, (h + 1) * LANES)
                slope = -neg_slope(h)
                extra = jnp.where(lane == 0, slope * base, jnp.where(lane == 1, slope * s_row, 0.0))
                kaug_ref[h, j, :, 0:LANES] = k_ref[0, pl.ds(start, t), cols]
                kaug_ref[h, j, :, LANES:2 * LANES] = extra.astype(BF16)
            return carry

        lax.fori_loop(0, nkb, body, 0)

        def vt_body(jj, carry):
            start = pl.multiple_of(jj * 2 * t, 2 * t)
            for h in range(n_heads):
                vt_ref[h, jj] = v_ref[0, pl.ds(start, 2 * t), h * LANES:(h + 1) * LANES].astype(F32).T.astype(BF16)
            return carry

        lax.fori_loop(0, nkb // 2, vt_body, 0)
        for h in range(n_heads):
            qt_ref[h, LANES:2 * LANES, :] = ones_rows

    for h in range(n_heads):
        qt_ref[h, 0:LANES, :] = _masked_qt(q_ref[0, :, h * LANES:(h + 1) * LANES])

    s_loc = lax.broadcasted_iota(jnp.int32, (t, tq), 0)
    q_loc = lax.broadcasted_iota(jnp.int32, (t, tq), 1)

    def scores_into(slot, jj):
        for h in range(n_heads):
            for c in range(2):
                ss_ref[slot, h, c] = jnp.dot(kaug_ref[h, 2 * jj + c], qt_ref[h], preferred_element_type=F32)

    def stage(jj, slot, own=False):
        other = 1 - slot
        prod = [jnp.dot(vt_ref[h, jnp.maximum(jj - 1, 0)], p_ref[other, h], preferred_element_type=F32)
                for h in range(n_heads)]
        if not own:
            scores_into(other, jj + 1)

        def block_scores(h, c, cols):
            a = ss_ref[slot, h, c, :, cols]
            if own:
                s_pos = s_loc + c * t
                d_loc = (q_loc - s_pos).astype(F32)
                a = a + jnp.where(d_loc < 0.0, (-2.0 * neg_slope(h)) * d_loc, 0.0)
                a = jnp.where((s_pos // CHUNK) <= (q_loc // CHUNK), a, NEG_INF)
            return a

        for h in range(n_heads):
            for half in range(2):
                cols = slice(half * tq, (half + 1) * tq)
                m_old = m_ref[h, :, cols]
                m_new = m_old
                for c in range(2):
                    m_new = jnp.maximum(m_new, jnp.max(block_scores(h, c, cols), axis=0, keepdims=True))
                alpha = jnp.exp(m_old - m_new)
                total = alpha * l_ref[h, :, cols]
                for c in range(2):
                    p = jnp.exp(block_scores(h, c, cols) - m_new)
                    total = total + jnp.sum(p, axis=0, keepdims=True)
                    p_ref[slot, h, c * t:(c + 1) * t, cols] = p.astype(BF16)
                l_ref[h, :, cols] = total
                m_ref[h, :, cols] = m_new
                acc_ref[h, :, cols] = alpha * (acc_ref[h, :, cols] + prod[h][:, cols])
        if own:
            for h in range(n_heads):
                acc_ref[h] += jnp.dot(vt_ref[h, jj], p_ref[slot, h], preferred_element_type=F32)

    acc_ref[...] = jnp.zeros_like(acc_ref)
    l_ref[...] = jnp.zeros_like(l_ref)
    m_ref[...] = jnp.full_like(m_ref, NEG_INF)
    p_ref[1] = jnp.zeros_like(p_ref[1])
    scores_into(0, 0)

    def body(i, carry):
        stage(2 * i, 0)
        stage(2 * i + 1, 1)
        return carry

    lax.fori_loop(0, lax.shift_right_logical(qi, 1), body, 0)

    @pl.when(qi % 2 == 1)
    def _():
        stage(qi - 1, 0)
        stage(qi, 1, own=True)

    @pl.when(qi % 2 == 0)
    def _():
        stage(qi, 0, own=True)

    lambda_init = scal_ref[0]
    lam = (jnp.exp(jnp.sum(lq1_ref[...] * lk1_ref[...], axis=-1, keepdims=True))
           - jnp.exp(jnp.sum(lq2_ref[...] * lk2_ref[...], axis=-1, keepdims=True)) + lambda_init)
    for h in range(n_heads):
        both = acc_ref[h] * (1.0 / l_ref[h])
        o_t = both[:, 0:tq] - lam * both[:, tq:2 * tq]
        o_ref[0, :, h * LANES:(h + 1) * LANES] = (_rms(o_t.T, g_ref[...]) * scal_ref[1]).astype(BF16)


def _da_attention(proj, scal, lq1, lk1, lq2, lk2, g, *, t=128, n_heads=DA_HEADS):
    b, s, _ = proj.shape
    nkb = s // t
    tq = 2 * t
    w = n_heads * LANES
    groups = DA_HEADS // n_heads
    q_off = (3 * SB_PAIRS * LANES) // w
    k_off = q_off + groups
    v_off = k_off + groups
    vec = lambda n: pl.BlockSpec((1, n), lambda bg, qi: (0, 0))
    return pl.pallas_call(
        functools.partial(_da_kernel, t=t, n_heads=n_heads,
                          head0_of_group=tuple(gi * n_heads for gi in range(groups))),
        grid=(b * groups, s // tq),
        in_specs=[
            pl.BlockSpec(memory_space=pltpu.SMEM),
            vec(HEAD_DIM), vec(HEAD_DIM), vec(HEAD_DIM), vec(HEAD_DIM), vec(LANES),
            pl.BlockSpec((1, tq, w), lambda bg, qi: (bg // groups, qi, q_off + bg % groups)),
            pl.BlockSpec((1, s, w), lambda bg, qi: (bg // groups, 0, k_off + bg % groups)),
            pl.BlockSpec((1, s, w), lambda bg, qi: (bg // groups, 0, v_off + bg % groups)),
        ],
        out_specs=pl.BlockSpec((1, tq, w), lambda bg, qi: (bg // groups, qi, bg % groups)),
        out_shape=jax.ShapeDtypeStruct((b, s, DA_HEADS * LANES), BF16),
        scratch_shapes=[
            pltpu.VMEM((n_heads, nkb // 2, LANES, 2 * t), BF16),
            pltpu.VMEM((n_heads, 2 * LANES, 2 * tq), BF16),
            pltpu.VMEM((n_heads, LANES, 2 * tq), F32),
            pltpu.VMEM((n_heads, 1, 2 * tq), F32),
            pltpu.VMEM((n_heads, 1, 2 * tq), F32),
            pltpu.VMEM((n_heads, nkb, t, 2 * LANES), BF16),
            pltpu.VMEM((2, n_heads, 2, t, 2 * tq), F32),
            pltpu.VMEM((2, n_heads, 2 * t, 2 * tq), BF16),
        ],
        compiler_params=pltpu.CompilerParams(
            dimension_semantics=("arbitrary", "arbitrary"), vmem_limit_bytes=VMEM_LIMIT_BYTES),
        name="da_attention",
    )(scal, lq1.reshape(1, -1), lk1.reshape(1, -1), lq2.reshape(1, -1), lk2.reshape(1, -1),
      g.reshape(1, -1), proj, proj, proj)


def _out_ffn_kernel(x_ref, sb_ref, da_ref, mod_ref, wout_ref, apost_ref, fpre_ref, wup_ref, cw_ref, cb_ref,
                    wdown_ref, fpost_ref, o_ref, carry_ref, ubuf_ref, act_ref, *, tm, fc, halo):
    i = pl.program_id(1)
    d_ff = wdown_ref.shape[0]

    @pl.when(i == 0)
    def _():
        carry_ref[...] = jnp.zeros_like(carry_ref)

    m = mod_ref[0]
    n_sb = sb_ref.shape[2]
    mixed = (jnp.dot(sb_ref[0], wout_ref[0:n_sb, :], preferred_element_type=F32)
             + jnp.dot(da_ref[0], wout_ref[n_sb:, :], preferred_element_type=F32))
    x1 = x_ref[0] + m[2:3] * _rms(mixed, apost_ref[...])
    h2 = (_rms(x1, fpre_ref[...]) * (1.0 + m[4:5]) + m[3:4]).astype(BF16)

    for ci in range(d_ff // fc):
        halves = []
        for part in range(2):
            col = part * d_ff + ci * fc
            u = jnp.dot(h2, wup_ref[:, col:col + fc], preferred_element_type=F32)
            ubuf_ref[0:halo, :] = carry_ref[:, col:col + fc]
            ubuf_ref[halo:halo + tm, :] = u
            carry_ref[:, col:col + fc] = u[tm - halo:tm]
            cw = cw_ref[:, col:col + fc]
            y = cb_ref[:, col:col + fc] + cw[CONV_WIDTH - 1:CONV_WIDTH] * u
            for tap in range(CONV_WIDTH - 1):
                back = CONV_WIDTH - 1 - tap
                y = y + cw[tap:tap + 1] * ubuf_ref[pl.ds(halo - back, tm), :]
            halves.append(y)
        gate, val = halves
        act_ref[:, ci * fc:(ci + 1) * fc] = (gate * jax.nn.sigmoid(gate) * val).astype(BF16)
    f = jnp.dot(act_ref[...], wdown_ref[...], preferred_element_type=F32)
    o_ref[0] = x1 + m[5:6] * _rms(f, fpost_ref[...])


def _out_ffn(x, sb, da, mod_l, w_out, apost, fpre, w_up, conv_w, conv_b, w_down, fpost, *, tm=512, fc=256):
    b, s, d = x.shape
    d_ff = w_down.shape[0]
    halo = 8
    const = lambda shape: pl.BlockSpec(shape, lambda bi, i: (0,) * len(shape), pipeline_mode=pl.Buffered(1))
    row = lambda n: pl.BlockSpec((1, tm, n), lambda bi, i: (bi, i, 0))
    return pl.pallas_call(
        functools.partial(_out_ffn_kernel, tm=tm, fc=fc, halo=halo),
        grid=(b, s // tm),
        in_specs=[
            row(d), row(sb.shape[2]), row(da.shape[2]),
            pl.BlockSpec((1, N_MOD, d), lambda bi, i: (bi, 0, 0)),
            const(w_out.shape), const((1, d)), const((1, d)),
            const(w_up.shape), const(conv_w.shape), const((1, 2 * d_ff)),
            const(w_down.shape), const((1, d)),
        ],
        out_specs=row(d),
        out_shape=jax.ShapeDtypeStruct((b, s, d), F32),
        scratch_shapes=[
            pltpu.VMEM((halo, 2 * d_ff), F32),
            pltpu.VMEM((halo + tm, fc), F32),
            pltpu.VMEM((tm, d_ff), BF16),
        ],
        compiler_params=pltpu.CompilerParams(
            dimension_semantics=("arbitrary", "arbitrary"), vmem_limit_bytes=VMEM_LIMIT_BYTES),
        name="out_ffn",
    )(x, sb, da, mod_l, w_out, apost.reshape(1, d), fpre.reshape(1, d), w_up, conv_w,
      conv_b.reshape(1, 2 * d_ff), w_down, fpost.reshape(1, d))


def kernel(x, c, ada_w, ada_b, attn_pre_g, attn_post_g, w_in, w_out, lambda_q1, lambda_k1, lambda_q2, lambda_k2,
           da_subln_g, ffn_pre_g, ffn_post_g, w_up, conv_w, conv_b, w_down):
    depth = ada_w.shape[0]
    b, s, d = x.shape
    mod = _ada_mod(c, ada_w, ada_b).reshape(depth, b, N_MOD, d)
    w_in, w_out, w_up, w_down = (w.astype(BF16) for w in (w_in, w_out, w_up, w_down))
    for l in range(depth):
        proj = _in_proj(x, mod[l], attn_pre_g[l], w_in[l])
        sb = _sb_attention(proj)
        lambda_init = 0.8 - 0.6 * math.exp(-0.3 * l)
        scal = jnp.array([lambda_init, 1.0 - lambda_init], F32)
        da = _da_attention(proj, scal, lambda_q1[l], lambda_k1[l], lambda_q2[l], lambda_k2[l], da_subln_g[l])
        x = _out_ffn(x, sb, da, mod[l], w_out[l], attn_post_g[l], ffn_pre_g[l], w_up[l], conv_w[l], conv_b[l],
                     w_down[l], ffn_post_g[l])
    return x
```

```python
import functools
import math

import jax
import jax.numpy as jnp
from jax import lax
from jax.experimental import pallas as pl
from jax.experimental.pallas import tpu as pltpu

F32 = jnp.float32
BF16 = jnp.bfloat16

CHUNK = 64
SB_HEADS = 8
HEAD_DIM = 64
DA_HEADS = 4
N_MOD = 6
CONV_WIDTH = 3
EPS = 1e-6
NEG_INF = -1e30

LANES = 128
SB_PAIRS = SB_HEADS // 2
VMEM_LIMIT_BYTES = 56 * 1024 * 1024

SB_SKIP_THRESHOLD = 110.0
SB_SKEW = 2


def _rms(x, g):
    return x * lax.rsqrt(jnp.mean(x * x, axis=-1, keepdims=True) + EPS) * g


def _ada_kernel(c_ref, w_ref, b_ref, o_ref):
    c = c_ref[...]
    c_act = (c * jax.nn.sigmoid(c)).astype(BF16)
    o_ref[0] = jnp.dot(c_act, w_ref[0].astype(BF16), preferred_element_type=F32) + b_ref[0]


def _ada_mod(c, ada_w, ada_b, *, tn=1536):
    depth, d, n = ada_w.shape
    b = c.shape[0]
    return pl.pallas_call(
        _ada_kernel,
        grid=(depth, n // tn),
        in_specs=[
            pl.BlockSpec((b, d), lambda l, j: (0, 0)),
            pl.BlockSpec((1, d, tn), lambda l, j: (l, 0, j)),
            pl.BlockSpec((1, 1, tn), lambda l, j: (l, 0, j)),
        ],
        out_specs=pl.BlockSpec((1, b, tn), lambda l, j: (l, 0, j)),
        out_shape=jax.ShapeDtypeStruct((depth, b, n), F32),
        compiler_params=pltpu.CompilerParams(
            dimension_semantics=("arbitrary", "arbitrary"), vmem_limit_bytes=VMEM_LIMIT_BYTES),
        name="ada_mod",
    )(c, ada_w, ada_b.reshape(depth, 1, n))


def _in_proj_kernel(x_ref, mod_ref, g_ref, w_ref, o_ref, *, n_chunk):
    m = mod_ref[0]
    h = (_rms(x_ref[0], g_ref[...]) * (1.0 + m[1:2]) + m[0:1]).astype(BF16)
    n = w_ref.shape[1]
    for n0 in range(0, n, n_chunk):
        o_ref[0, :, n0:n0 + n_chunk] = jnp.dot(
            h, w_ref[:, n0:n0 + n_chunk], preferred_element_type=F32).astype(BF16)


def _in_proj(x, mod_l, g, w, *, tm=512, n_chunk=512):
    b, s, d = x.shape
    n = w.shape[1]
    return pl.pallas_call(
        functools.partial(_in_proj_kernel, n_chunk=n_chunk),
        grid=(b, s // tm),
        in_specs=[
            pl.BlockSpec((1, tm, d), lambda bi, i: (bi, i, 0)),
            pl.BlockSpec((1, N_MOD, d), lambda bi, i: (bi, 0, 0)),
            pl.BlockSpec((1, d), lambda bi, i: (0, 0)),
            pl.BlockSpec((d, n), lambda bi, i: (0, 0)),
        ],
        out_specs=pl.BlockSpec((1, tm, n), lambda bi, i: (bi, i, 0)),
        out_shape=jax.ShapeDtypeStruct((b, s, n), BF16),
        compiler_params=pltpu.CompilerParams(
            dimension_semantics=("arbitrary", "arbitrary"), vmem_limit_bytes=VMEM_LIMIT_BYTES),
        name="in_proj",
    )(x, mod_l, g.reshape(1, d), w)


def _split_halves_t(blk):
    bt = blk.astype(F32).T
    row = lax.broadcasted_iota(jnp.int32, bt.shape, 0)
    return jnp.where(row < HEAD_DIM, bt, 0.0), jnp.where(row >= HEAD_DIM, bt, 0.0)


def _masked_qt(q_blk):
    lo, hi = _split_halves_t(q_blk * jnp.asarray(HEAD_DIM ** -0.5, BF16))
    return jnp.concatenate([lo, hi], axis=1).astype(BF16)


def _sb_kernel(q_ref, k_ref, v_ref, o_ref, vt_ref, qt_ref, acc_ref, c_ref, zz_ref, ws_ref, *, t, n_pairs):
    qi = pl.program_id(1)
    nkb = vt_ref.shape[1]

    @pl.when(qi == 0)
    def _():
        def body(j, carry):
            start = pl.multiple_of(j * t, t)
            for p in range(n_pairs):
                lo, hi = _split_halves_t(v_ref[0, pl.ds(start, t), p * LANES:(p + 1) * LANES])
                vt_ref[p, j] = jnp.concatenate([lo, hi], axis=1).astype(BF16)
            return carry

        lax.fori_loop(0, nkb, body, 0)

    for p in range(n_pairs):
        qt_ref[p] = _masked_qt(q_ref[0, :, p * LANES:(p + 1) * LANES])
    acc_ref[...] = jnp.zeros_like(acc_ref)
    c_ref[...] = jnp.zeros_like(c_ref)

    tq = 2 * t
    s_loc = lax.broadcasted_iota(jnp.int32, (t, tq), 0)
    q_loc = lax.broadcasted_iota(jnp.int32, (t, tq), 1)
    tri = (lax.broadcasted_iota(jnp.int32, (t, t), 1)
           > lax.broadcasted_iota(jnp.int32, (t, t), 0)).astype(BF16)
    n_heads = 2 * n_pairs

    def scores_into(slot, j):
        start = pl.multiple_of(j * t, t)
        for p in range(n_pairs):
            zz_ref[slot, p] = jnp.dot(k_ref[0, pl.ds(start, t), p * LANES:(p + 1) * LANES], qt_ref[p],
                                      preferred_element_type=F32)

    def stage(j, slot, key_off=None):
        other = 1 - slot
        c_in = [c_ref[p] for p in range(n_pairs)]
        smallest = jnp.min(functools.reduce(jnp.minimum, c_in))
        causal = None if key_off is None else (s_loc + key_off) < q_loc
        cols_of = lambda i: slice((i % 2) * tq, (i % 2 + 1) * tq)
        suffix, first_row, prod = [None] * n_heads, [None] * n_heads, None
        for step in range(n_heads + SB_SKEW):
            if step < n_heads:
                z = zz_ref[slot, step // 2, :, cols_of(step)]
                sp = jnp.where(z > 80.0, z, jnp.log(1.0 + jnp.exp(z)))
                if causal is not None:
                    sp = jnp.where(causal, sp, 0.0)
                zz_ref[slot, step // 2, :, cols_of(step)] = z - sp
                first_row[step] = sp[0:1, :]
                suffix[step] = jnp.dot(tri, sp.astype(BF16), preferred_element_type=F32)
            if step == 0:
                prod = [jnp.dot(vt_ref[p, jnp.minimum(j + 1, nkb - 1)], ws_ref[other, p],
                                preferred_element_type=F32) for p in range(n_pairs)]
                scores_into(other, jnp.maximum(j - 1, 0))
            i = step - SB_SKEW
            if i >= 0:
                cs = suffix[i] + c_in[i // 2][:, cols_of(i)]
                w = jnp.exp(zz_ref[slot, i // 2, :, cols_of(i)] - cs)
                if causal is not None:
                    w = jnp.where(causal, w, 0.0)
                ws_ref[slot, i // 2, (i % 2) * t:(i % 2 + 1) * t, :] = w.astype(BF16)
                c_ref[i // 2, :, cols_of(i)] = cs[0:1, :] + first_row[i]
        for p in range(n_pairs):
            acc_ref[p] += prod[p]
        return smallest

    ws_ref[...] = jnp.zeros_like(ws_ref)

    scores_into(1, 2 * qi + 1)
    stage(2 * qi + 1, 1, key_off=t)
    smallest = stage(2 * qi, 0, key_off=0)

    def cond(state):
        j, smallest = state
        return jnp.logical_and(j >= 1, smallest < SB_SKIP_THRESHOLD)

    def body(state):
        j = state[0]
        stage(j, 1)
        return j - 2, stage(j - 1, 0)

    j_end, _ = lax.while_loop(cond, body, (2 * qi - 1, smallest))
    last = j_end + 1
    for p in range(n_pairs):
        acc_ref[p] += jnp.dot(vt_ref[p, last], ws_ref[0, p], preferred_element_type=F32)

    for p in range(n_pairs):
        o_ref[0, :, p * LANES:(p + 1) * LANES] = acc_ref[p].T.astype(BF16)


def _sb_attention(proj, *, t=128, n_pairs=SB_PAIRS):
    b, s, _ = proj.shape
    nkb = s // t
    tq = 2 * t
    w = n_pairs * LANES
    groups = SB_PAIRS // n_pairs
    return pl.pallas_call(
        functools.partial(_sb_kernel, t=t, n_pairs=n_pairs),
        grid=(b * groups, s // tq),
        in_specs=[
            pl.BlockSpec((1, tq, w), lambda bg, qi: (bg // groups, qi, bg % groups)),
            pl.BlockSpec((1, s, w), lambda bg, qi: (bg // groups, 0, groups + bg % groups)),
            pl.BlockSpec((1, s, w), lambda bg, qi: (bg // groups, 0, 2 * groups + bg % groups)),
        ],
        out_specs=pl.BlockSpec((1, tq, w), lambda bg, qi: (bg // groups, qi, bg % groups)),
        out_shape=jax.ShapeDtypeStruct((b, s, SB_PAIRS * LANES), BF16),
        scratch_shapes=[
            pltpu.VMEM((n_pairs, nkb, LANES, 2 * t), BF16),
            pltpu.VMEM((n_pairs, LANES, 2 * tq), BF16),
            pltpu.VMEM((n_pairs, LANES, tq), F32),
            pltpu.VMEM((n_pairs, 1, 2 * tq), F32),
            pltpu.VMEM((2, n_pairs, t, 2 * tq), F32),
            pltpu.VMEM((2, n_pairs, 2 * t, tq), BF16),
        ],
        compiler_params=pltpu.CompilerParams(
            dimension_semantics=("arbitrary", "arbitrary"), vmem_limit_bytes=VMEM_LIMIT_BYTES),
        name="sb_attention",
    )(proj, proj, proj)


def _da_kernel(scal_ref, lq1_ref, lk1_ref, lq2_ref, lk2_ref, g_ref, q_ref, k_ref, v_ref, o_ref,
               vt_ref, qt_ref, acc_ref, m_ref, l_ref, kaug_ref, ss_ref, p_ref, *, t, n_heads, head0_of_group):
    qi = pl.program_id(1)
    nkb = kaug_ref.shape[1]
    tq = 2 * t
    group = pl.program_id(0) % len(head0_of_group)

    def neg_slope(h):
        vals = [-(2.0 ** (-8.0 * (h0 + h + 1) / DA_HEADS)) for h0 in head0_of_group]
        out = jnp.float32(vals[-1])
        for gi in range(len(vals) - 2, -1, -1):
            out = jnp.where(group == gi, jnp.float32(vals[gi]), out)
        return out

    @pl.when(qi == 0)
    def _():
        lane = lax.broadcasted_iota(jnp.int32, (t, LANES), 1)
        s_row = lax.broadcasted_iota(jnp.int32, (t, LANES), 0).astype(F32)
        ones_rows = (lax.broadcasted_iota(jnp.int32, (LANES, 2 * tq), 0) < 2).astype(BF16)

        def body(j, carry):
            start = pl.multiple_of(j * t, t)
            base = jnp.asarray(j * t, F32)
            for h in range(n_heads):
                cols = slice(h * LANES, (h + 1) * LANES)
                slope = -neg_slope(h)
                extra = jnp.where(lane == 0, slope * base, jnp.where(lane == 1, slope * s_row, 0.0))
                kaug_ref[h, j, :, 0:LANES] = k_ref[0, pl.ds(start, t), cols]
                kaug_ref[h, j, :, LANES:2 * LANES] = extra.astype(BF16)
            return carry

        lax.fori_loop(0, nkb, body, 0)

        def vt_body(jj, carry):
            start = pl.multiple_of(jj * 2 * t, 2 * t)
            for h in range(n_heads):
                vt_ref[h, jj] = v_ref[0, pl.ds(start, 2 * t), h * LANES:(h + 1) * LANES].astype(F32).T.astype(BF16)
            return carry

        lax.fori_loop(0, nkb // 2, vt_body, 0)
        for h in range(n_heads):
            qt_ref[h, LANES:2 * LANES, :] = ones_rows

    for h in range(n_heads):
        qt_ref[h, 0:LANES, :] = _masked_qt(q_ref[0, :, h * LANES:(h + 1) * LANES])

    s_loc = lax.broadcasted_iota(jnp.int32, (t, tq), 0)
    q_loc = lax.broadcasted_iota(jnp.int32, (t, tq), 1)

    def scores_into(slot, jj):
        for h in range(n_heads):
            for c in range(2):
                ss_ref[slot, h, c] = jnp.dot(kaug_ref[h, 2 * jj + c], qt_ref[h], preferred_element_type=F32)

    def stage(jj, slot, own=False):
        other = 1 - slot
        prod = [jnp.dot(vt_ref[h, jnp.maximum(jj - 1, 0)], p_ref[other, h], preferred_element_type=F32)
                for h in range(n_heads)]
        if not own:
            scores_into(other, jj + 1)

        def block_scores(h, c, cols):
            a = ss_ref[slot, h, c, :, cols]
            if own:
                s_pos = s_loc + c * t
                d_loc = (q_loc - s_pos).astype(F32)
                a = a + jnp.where(d_loc < 0.0, (-2.0 * neg_slope(h)) * d_loc, 0.0)
                a = jnp.where((s_pos // CHUNK) <= (q_loc // CHUNK), a, NEG_INF)
            return a

        for h in range(n_heads):
            for half in range(2):
                cols = slice(half * tq, (half + 1) * tq)
                m_old = m_ref[h, :, cols]
                m_new = m_old
                for c in range(2):
                    m_new = jnp.maximum(m_new, jnp.max(block_scores(h, c, cols), axis=0, keepdims=True))
                alpha = jnp.exp(m_old - m_new)
                total = alpha * l_ref[h, :, cols]
                for c in range(2):
                    p = jnp.exp(block_scores(h, c, cols) - m_new)
                    total = total + jnp.sum(p, axis=0, keepdims=True)
                    p_ref[slot, h, c * t:(c + 1) * t, cols] = p.astype(BF16)
                l_ref[h, :, cols] = total
                m_ref[h, :, cols] = m_new
                acc_ref[h, :, cols] = alpha * (acc_ref[h, :, cols] + prod[h][:, cols])
        if own:
            for h in range(n_heads):
                acc_ref[h] += jnp.dot(vt_ref[h, jj], p_ref[slot, h], preferred_element_type=F32)

    acc_ref[...] = jnp.zeros_like(acc_ref)
    l_ref[...] = jnp.zeros_like(l_ref)
    m_ref[...] = jnp.full_like(m_ref, NEG_INF)
    p_ref[1] = jnp.zeros_like(p_ref[1])
    scores_into(0, 0)

    def body(i, carry):
        stage(2 * i, 0)
        stage(2 * i + 1, 1)
        return carry

    lax.fori_loop(0, lax.shift_right_logical(qi, 1), body, 0)

    @pl.when(qi % 2 == 1)
    def _():
        stage(qi - 1, 0)
        stage(qi, 1, own=True)

    @pl.when(qi % 2 == 0)
    def _():
        stage(qi, 0, own=True)

    lambda_init = scal_ref[0]
    lam = (jnp.exp(jnp.sum(lq1_ref[...] * lk1_ref[...], axis=-1, keepdims=True))
           - jnp.exp(jnp.sum(lq2_ref[...] * lk2_ref[...], axis=-1, keepdims=True)) + lambda_init)
    for h in range(n_heads):
        both = acc_ref[h] * (1.0 / l_ref[h])
        o_t = both[:, 0:tq] - lam * both[:, tq:2 * tq]
        o_ref[0, :, h * LANES:(h + 1) * LANES] = (_rms(o_t.T, g_ref[...]) * scal_ref[1]).astype(BF16)


def _da_attention(proj, scal, lq1, lk1, lq2, lk2, g, *, t=128, n_heads=DA_HEADS):
    b, s, _ = proj.shape
    nkb = s // t
    tq = 2 * t
    w = n_heads * LANES
    groups = DA_HEADS // n_heads
    q_off = (3 * SB_PAIRS * LANES) // w
    k_off = q_off + groups
    v_off = k_off + groups
    vec = lambda n: pl.BlockSpec((1, n), lambda bg, qi: (0, 0))
    return pl.pallas_call(
        functools.partial(_da_kernel, t=t, n_heads=n_heads,
                          head0_of_group=tuple(gi * n_heads for gi in range(groups))),
        grid=(b * groups, s // tq),
        in_specs=[
            pl.BlockSpec(memory_space=pltpu.SMEM),
            vec(HEAD_DIM), vec(HEAD_DIM), vec(HEAD_DIM), vec(HEAD_DIM), vec(LANES),
            pl.BlockSpec((1, tq, w), lambda bg, qi: (bg // groups, qi, q_off + bg % groups)),
            pl.BlockSpec((1, s, w), lambda bg, qi: (bg // groups, 0, k_off + bg % groups)),
            pl.BlockSpec((1, s, w), lambda bg, qi: (bg // groups, 0, v_off + bg % groups)),
        ],
        out_specs=pl.BlockSpec((1, tq, w), lambda bg, qi: (bg // groups, qi, bg % groups)),
        out_shape=jax.ShapeDtypeStruct((b, s, DA_HEADS * LANES), BF16),
        scratch_shapes=[
            pltpu.VMEM((n_heads, nkb // 2, LANES, 2 * t), BF16),
            pltpu.VMEM((n_heads, 2 * LANES, 2 * tq), BF16),
            pltpu.VMEM((n_heads, LANES, 2 * tq), F32),
            pltpu.VMEM((n_heads, 1, 2 * tq), F32),
            pltpu.VMEM((n_heads, 1, 2 * tq), F32),
            pltpu.VMEM((n_heads, nkb, t, 2 * LANES), BF16),
            pltpu.VMEM((2, n_heads, 2, t, 2 * tq), F32),
            pltpu.VMEM((2, n_heads, 2 * t, 2 * tq), BF16),
        ],
        compiler_params=pltpu.CompilerParams(
            dimension_semantics=("arbitrary", "arbitrary"), vmem_limit_bytes=VMEM_LIMIT_BYTES),
        name="da_attention",
    )(scal, lq1.reshape(1, -1), lk1.reshape(1, -1), lq2.reshape(1, -1), lk2.reshape(1, -1),
      g.reshape(1, -1), proj, proj, proj)


def _out_ffn_kernel(x_ref, sb_ref, da_ref, mod_ref, wout_ref, apost_ref, fpre_ref, wup_ref, cw_ref, cb_ref,
                    wdown_ref, fpost_ref, o_ref, carry_ref, ubuf_ref, act_ref, *, tm, fc, halo):
    i = pl.program_id(1)
    d_ff = wdown_ref.shape[0]

    @pl.when(i == 0)
    def _():
        carry_ref[...] = jnp.zeros_like(carry_ref)

    m = mod_ref[0]
    n_sb = sb_ref.shape[2]
    mixed = (jnp.dot(sb_ref[0], wout_ref[0:n_sb, :], preferred_element_type=F32)
             + jnp.dot(da_ref[0], wout_ref[n_sb:, :], preferred_element_type=F32))
    x1 = x_ref[0] + m[2:3] * _rms(mixed, apost_ref[...])
    h2 = (_rms(x1, fpre_ref[...]) * (1.0 + m[4:5]) + m[3:4]).astype(BF16)

    for ci in range(d_ff // fc):
        halves = []
        for part in range(2):
            col = part * d_ff + ci * fc
            u = jnp.dot(h2, wup_ref[:, col:col + fc], preferred_element_type=F32)
            ubuf_ref[0:halo, :] = carry_ref[:, col:col + fc]
            ubuf_ref[halo:halo + tm, :] = u
            carry_ref[:, col:col + fc] = u[tm - halo:tm]
            cw = cw_ref[:, col:col + fc]
            y = cb_ref[:, col:col + fc] + cw[CONV_WIDTH - 1:CONV_WIDTH] * u
            for tap in range(CONV_WIDTH - 1):
                back = CONV_WIDTH - 1 - tap
                y = y + cw[tap:tap + 1] * ubuf_ref[pl.ds(halo - back, tm), :]
            halves.append(y)
        gate, val = halves
        act_ref[:, ci * fc:(ci + 1) * fc] = (gate * jax.nn.sigmoid(gate) * val).astype(BF16)
    f = jnp.dot(act_ref[...], wdown_ref[...], preferred_element_type=F32)
    o_ref[0] = x1 + m[5:6] * _rms(f, fpost_ref[...])


def _out_ffn(x, sb, da, mod_l, w_out, apost, fpre, w_up, conv_w, conv_b, w_down, fpost, *, tm=512, fc=256):
    b, s, d = x.shape
    d_ff = w_down.shape[0]
    halo = 8
    const = lambda shape: pl.BlockSpec(shape, lambda bi, i: (0,) * len(shape), pipeline_mode=pl.Buffered(1))
    row = lambda n: pl.BlockSpec((1, tm, n), lambda bi, i: (bi, i, 0))
    return pl.pallas_call(
        functools.partial(_out_ffn_kernel, tm=tm, fc=fc, halo=halo),
        grid=(b, s // tm),
        in_specs=[
            row(d), row(sb.shape[2]), row(da.shape[2]),
            pl.BlockSpec((1, N_MOD, d), lambda bi, i: (bi, 0, 0)),
            const(w_out.shape), const((1, d)), const((1, d)),
            const(w_up.shape), const(conv_w.shape), const((1, 2 * d_ff)),
            const(w_down.shape), const((1, d)),
        ],
        out_specs=row(d),
        out_shape=jax.ShapeDtypeStruct((b, s, d), F32),
        scratch_shapes=[
            pltpu.VMEM((halo, 2 * d_ff), F32),
            pltpu.VMEM((halo + tm, fc), F32),
            pltpu.VMEM((tm, d_ff), BF16),
        ],
        compiler_params=pltpu.CompilerParams(
            dimension_semantics=("arbitrary", "arbitrary"), vmem_limit_bytes=VMEM_LIMIT_BYTES),
        name="out_ffn",
    )(x, sb, da, mod_l, w_out, apost.reshape(1, d), fpre.reshape(1, d), w_up, conv_w,
      conv_b.reshape(1, 2 * d_ff), w_down, fpost.reshape(1, d))


def kernel(x, c, ada_w, ada_b, attn_pre_g, attn_post_g, w_in, w_out, lambda_q1, lambda_k1, lambda_q2, lambda_k2,
           da_subln_g, ffn_pre_g, ffn_post_g, w_up, conv_w, conv_b, w_down):
    depth = ada_w.shape[0]
    b, s, d = x.shape
    mod = _ada_mod(c, ada_w, ada_b).reshape(depth, b, N_MOD, d)
    w_in, w_out, w_up, w_down = (w.astype(BF16) for w in (w_in, w_out, w_up, w_down))
    for l in range(depth):
        proj = _in_proj(x, mod[l], attn_pre_g[l], w_in[l])
        sb = _sb_attention(proj)
        lambda_init = 0.8 - 0.6 * math.exp(-0.3 * l)
        scal = jnp.array([lambda_init, 1.0 - lambda_init], F32)
        da = _da_attention(proj, scal, lambda_q1[l], lambda_k1[l], lambda_q2[l], lambda_k2[l], da_subln_g[l])
        x = _out_ffn(x, sb, da, mod[l], w_out[l], attn_post_g[l], ffn_pre_g[l], w_up[l], conv_w[l], conv_b[l],
                     w_down[l], ffn_post_g[l])
    return x
```
